```python
import math
import jax, jax.numpy as jnp
from jax import lax
import numpy as np

D_MODEL = 1024
BATCH = 4
SEQ = 4096
DEPTH = 2
DEC_BATCH = 32
DEC_SEQ = 1
PAST_LEN = 16384
PAGE_SIZE = 128

N_MIXERS = 2
N_S5_LAYERS = (DEPTH + 1) // 2
N_ATT_LAYERS = DEPTH // 2
S5_GROUP = 16
S5_GROUPS = D_MODEL // S5_GROUP
S5_STATE = 64
DT_MIN = 1e-3
DT_MAX = 1e-1
N_HEADS = 8
N_SUB = 2 * N_HEADS
HEAD_DIM = D_MODEL // N_SUB
V_DIM = 2 * HEAD_DIM
ATTN_SCALE = HEAD_DIM ** -0.5
ROPE_THETA = 10000.0
Q_BLOCK = 128
D_FF = 2816
EPS = 1e-6
NEG_INF = -1e30

kernel_name = 'hybrid_s5_diffattn_macaron_step'


def rmsnorm(x, g):
    xf = x.astype(jnp.float32)
    y = xf * lax.rsqrt(jnp.mean(xf * xf, axis=-1, keepdims=True) + EPS)
    return (y * g.astype(jnp.float32)).astype(x.dtype)


def swiglu(x, w_in, w_out):
    g, u = jnp.split(x @ w_in, 2, axis=-1)
    return ((jax.nn.silu(g) * u) @ w_out).astype(x.dtype)


def rope(x, pos):
    half = HEAD_DIM // 2
    inv = ROPE_THETA ** (-jnp.arange(half, dtype=jnp.float32) / half)
    ang = pos.astype(jnp.float32)[:, None] * inv[None, :]
    cos = jnp.cos(ang)[:, None, :]
    sin = jnp.sin(ang)[:, None, :]
    xf = x.astype(jnp.float32)
    x1, x2 = xf[..., :half], xf[..., half:]
    return jnp.concatenate([x1 * cos - x2 * sin, x2 * cos + x1 * sin], axis=-1).astype(x.dtype)


def s5_discretise(lam_re, lam_im, log_dt, b_re, b_im):
    dt = jnp.exp(log_dt.astype(jnp.float32))[:, None]
    lr = lam_re.astype(jnp.float32)
    li = lam_im.astype(jnp.float32)
    mag = jnp.exp(lr * dt)
    a_re = mag * jnp.cos(li * dt)
    a_im = mag * jnp.sin(li * dt)
    den = lr * lr + li * li
    nr = a_re - 1.0
    z_re = (nr * lr + a_im * li) / den
    z_im = (a_im * lr - nr * li) / den
    br = b_re.astype(jnp.float32)
    bi = b_im.astype(jnp.float32)
    bb_re = z_re[..., None] * br - z_im[..., None] * bi
    bb_im = z_re[..., None] * bi + z_im[..., None] * br
    return a_re, a_im, bb_re, bb_im


def s5_combine(e1, e2):
    a1r, a1i, b1r, b1i = e1
    a2r, a2i, b2r, b2i = e2
    return (a2r * a1r - a2i * a1i, a2r * a1i + a2i * a1r,
            a2r * b1r - a2i * b1i + b2r, a2r * b1i + a2i * b1r + b2i)


def s5_mixer(u, h0_re, h0_im, lam_re, lam_im, log_dt, b_re, b_im, c_re, c_im, d, w_glu, b_glu):
    bsz, t = u.shape[:2]
    a_re, a_im, bb_re, bb_im = s5_discretise(lam_re, lam_im, log_dt, b_re, b_im)
    uf = u.astype(jnp.float32)
    ug = uf.reshape(bsz, t, S5_GROUPS, S5_GROUP)
    bu_re = jnp.einsum('gnc,btgc->btgn', bb_re, ug)
    bu_im = jnp.einsum('gnc,btgc->btgn', bb_im, ug)
    ar = jnp.broadcast_to(a_re, bu_re.shape)
    ai = jnp.broadcast_to(a_im, bu_re.shape)
    cr, ci, hr, hi = lax.associative_scan(s5_combine, (ar, ai, bu_re, bu_im), axis=1)
    h0r = h0_re.astype(jnp.float32)[:, None]
    h0i = h0_im.astype(jnp.float32)[:, None]
    hr = hr + cr * h0r - ci * h0i
    hi = hi + cr * h0i + ci * h0r
    y = (jnp.einsum('gcn,btgn->btgc', c_re.astype(jnp.float32), hr)
         - jnp.einsum('gcn,btgn->btgc', c_im.astype(jnp.float32), hi))
    y = y.reshape(bsz, t, D_MODEL) + d.astype(jnp.float32) * uf
    z = jax.nn.gelu(y)
    o, g = jnp.split(z @ w_glu.astype(jnp.float32) + b_glu.astype(jnp.float32), 2, axis=-1)
    return (o * jax.nn.sigmoid(g)).astype(u.dtype), hr[:, -1], hi[:, -1]


def diff_attn_project(h, pos, w_qkv, q_norm, k_norm):
    bsz, t = h.shape[:2]
    q, k, v = jnp.split(h @ w_qkv, [D_MODEL, 2 * D_MODEL], axis=-1)
    q = rope(rmsnorm(q.reshape(bsz, t, N_SUB, HEAD_DIM), q_norm), pos)
    k = rope(rmsnorm(k.reshape(bsz, t, N_SUB, HEAD_DIM), k_norm), pos)
    v = v.reshape(bsz, t, N_HEADS, V_DIM)
    return q, k, v


def diff_attn_prompt(q, k, v):
    bsz, t = q.shape[:2]
    nb = t // Q_BLOCK
    kf = k.astype(jnp.float32)
    vf = v.astype(jnp.float32)
    kpos = jnp.arange(t)
    qb = q.astype(jnp.float32).reshape(bsz, nb, Q_BLOCK, N_SUB, HEAD_DIM).swapaxes(0, 1)

    def one_block(args):
        q_blk, start = args
        qpos = start + jnp.arange(Q_BLOCK)
        s = jnp.einsum('bqjd,bkjd->bjqk', q_blk, kf) * ATTN_SCALE
        s = jnp.where(kpos[None, :] <= qpos[:, None], s, NEG_INF)
        p = jax.nn.softmax(s, axis=-1).reshape(bsz, N_HEADS, 2, Q_BLOCK, t)
        return jnp.einsum('bhmqk,bkhe->bqhme', p, vf)

    o = lax.map(one_block, (qb, jnp.arange(nb) * Q_BLOCK))
    return o.swapaxes(0, 1).reshape(bsz, t, N_HEADS, 2, V_DIM)


def diff_attn_sample(q, k_new, v_new, cache_k, cache_v, layer, page_table):
    db, s_len = q.shape[:2]
    qf = q.astype(jnp.float32) * ATTN_SCALE

    def update(carry, s, v):
        m, l, acc = carry
        m_new = jnp.maximum(m, jnp.max(s, axis=-1))
        alpha = jnp.exp(m - m_new)
        p = jnp.exp(s - m_new[..., None])
        l_new = l * alpha + jnp.sum(p, axis=-1)
        pv = jnp.einsum('bhmqk,bkhe->bhmqe', p.reshape(db, N_HEADS, 2, s_len, -1), v.astype(jnp.float32))
        acc_new = acc * alpha.reshape(db, N_HEADS, 2, s_len, 1) + pv
        return (m_new, l_new, acc_new)

    def page_step(carry, phys):
        kp = cache_k[layer, phys]
        vp = cache_v[layer, phys]
        s = jnp.einsum('bqjd,bkjd->bjqk', qf, kp.astype(jnp.float32))
        return update(carry, s, vp), None

    init = (jnp.full((db, N_SUB, s_len), NEG_INF, jnp.float32),
            jnp.zeros((db, N_SUB, s_len), jnp.float32),
            jnp.zeros((db, N_HEADS, 2, s_len, V_DIM), jnp.float32))
    carry, _ = lax.scan(page_step, init, page_table.T)
    causal = jnp.tril(jnp.ones((s_len, s_len), dtype=bool))
    s = jnp.einsum('bqjd,bkjd->bjqk', qf, k_new.astype(jnp.float32))
    s = jnp.where(causal, s, NEG_INF)
    m, l, acc = update(carry, s, v_new)
    o = acc / l.reshape(db, N_HEADS, 2, s_len, 1)
    return o.transpose(0, 3, 1, 2, 4)


def diff_attn_merge(o, lam, lam_init, subln, w_o, dtype):
    out = o[..., 0, :] - lam * o[..., 1, :]
    out = rmsnorm(out, subln) * (1.0 - lam_init)
    bsz, t = out.shape[:2]
    return (out.reshape(bsz, t, D_MODEL).astype(dtype) @ w_o).astype(dtype)


def setup_inputs(seed: int = 0) -> dict:
    key = jax.random.key(seed)
    ks = jax.random.split(key, 32)
    f32 = jnp.float32
    n_pages = PAST_LEN // PAGE_SIZE
    n_phys = (DEC_BATCH * n_pages * 5) // 4

    def nrm(k, shape, scale):
        return scale * jax.random.normal(k, shape, f32)

    page_table = jax.random.permutation(ks[6], n_phys)[: DEC_BATCH * n_pages]
    page_table = page_table.reshape(DEC_BATCH, n_pages).astype(jnp.int32)
    return {
        'x_prompt': nrm(ks[0], (BATCH, SEQ, D_MODEL), 1.0),
        'x_sample': nrm(ks[1], (DEC_BATCH, DEC_SEQ, D_MODEL), 1.0),
        'state_s5_re': nrm(ks[2], (N_S5_LAYERS, DEC_BATCH, S5_GROUPS, S5_STATE), 0.1),
        'state_s5_im': nrm(ks[3], (N_S5_LAYERS, DEC_BATCH, S5_GROUPS, S5_STATE), 0.1),
        'cache_k': nrm(ks[4], (N_ATT_LAYERS, n_phys, PAGE_SIZE, N_SUB, HEAD_DIM), 1.0),
        'cache_v': nrm(ks[5], (N_ATT_LAYERS, n_phys, PAGE_SIZE, N_HEADS, V_DIM), 1.0),
        'page_table': page_table,
        'ffn_norm': 1.0 + nrm(ks[7], (DEPTH, 2, D_MODEL), 0.01),
        'ffn_w_in': nrm(ks[8], (DEPTH, 2, D_MODEL, 2 * D_FF), D_MODEL ** -0.5),
        'ffn_w_out': nrm(ks[9], (DEPTH, 2, D_FF, D_MODEL), D_FF ** -0.5),
        'mix_norm': 1.0 + nrm(ks[10], (DEPTH, D_MODEL), 0.01),
        's5_lambda_re': -0.5 + nrm(ks[11], (N_S5_LAYERS, S5_GROUPS, S5_STATE), 0.01),
        's5_lambda_im': jnp.pi * jnp.arange(S5_STATE, dtype=f32) + nrm(ks[12], (N_S5_LAYERS, S5_GROUPS, S5_STATE), 0.01),
        's5_log_dt': jax.random.uniform(ks[13], (N_S5_LAYERS, S5_GROUPS), f32, minval=math.log(DT_MIN), maxval=math.log(DT_MAX)),
        's5_b_re': nrm(ks[14], (N_S5_LAYERS, S5_GROUPS, S5_STATE, S5_GROUP), (2 * S5_GROUP) ** -0.5),
        's5_b_im': nrm(ks[15], (N_S5_LAYERS, S5_GROUPS, S5_STATE, S5_GROUP), (2 * S5_GROUP) ** -0.5),
        's5_c_re': nrm(ks[16], (N_S5_LAYERS, S5_GROUPS, S5_GROUP, S5_STATE), (2 * S5_STATE) ** -0.5),
        's5_c_im': nrm(ks[17], (N_S5_LAYERS, S5_GROUPS, S5_GROUP, S5_STATE), (2 * S5_STATE) ** -0.5),
        's5_d': nrm(ks[18], (N_S5_LAYERS, D_MODEL), 1.0),
        's5_w_glu': nrm(ks[19], (N_S5_LAYERS, D_MODEL, 2 * D_MODEL), D_MODEL ** -0.5),
        's5_b_glu': nrm(ks[20], (N_S5_LAYERS, 2 * D_MODEL), 0.01),
        'attn_w_qkv': nrm(ks[21], (N_ATT_LAYERS, D_MODEL, 3 * D_MODEL), D_MODEL ** -0.5),
        'attn_q_norm': 1.0 + nrm(ks[22], (N_ATT_LAYERS, HEAD_DIM), 0.01),
        'attn_k_norm': 1.0 + nrm(ks[23], (N_ATT_LAYERS, HEAD_DIM), 0.01),
        'attn_lambda_q1': nrm(ks[24], (N_ATT_LAYERS, HEAD_DIM), 0.1),
        'attn_lambda_k1': nrm(ks[25], (N_ATT_LAYERS, HEAD_DIM), 0.1),
        'attn_lambda_q2': nrm(ks[26], (N_ATT_LAYERS, HEAD_DIM), 0.1),
        'attn_lambda_k2': nrm(ks[27], (N_ATT_LAYERS, HEAD_DIM), 0.1),
        'attn_subln': 1.0 + nrm(ks[28], (N_ATT_LAYERS, V_DIM), 0.01),
        'attn_w_o': nrm(ks[29], (N_ATT_LAYERS, D_MODEL, D_MODEL), D_MODEL ** -0.5),
    }


def reference(x_prompt, x_sample, state_s5_re, state_s5_im, cache_k, cache_v, page_table,
              ffn_norm, ffn_w_in, ffn_w_out, mix_norm,
              s5_lambda_re, s5_lambda_im, s5_log_dt, s5_b_re, s5_b_im, s5_c_re, s5_c_im, s5_d, s5_w_glu, s5_b_glu,
              attn_w_qkv, attn_q_norm, attn_k_norm, attn_lambda_q1, attn_lambda_k1, attn_lambda_q2, attn_lambda_k2,
              attn_subln, attn_w_o):
    pos_p = jnp.arange(SEQ)
    pos_s = PAST_LEN + jnp.arange(DEC_SEQ)
    xp, xs = x_prompt, x_sample
    s5_re_p, s5_im_p, s5_re_s, s5_im_s = [], [], [], []
    k_p, v_p, k_s, v_s = [], [], [], []
    for i in range(DEPTH):
        xp = xp + 0.5 * swiglu(rmsnorm(xp, ffn_norm[i, 0]), ffn_w_in[i, 0], ffn_w_out[i, 0])
        xs = xs + 0.5 * swiglu(rmsnorm(xs, ffn_norm[i, 0]), ffn_w_in[i, 0], ffn_w_out[i, 0])
        hp = rmsnorm(xp, mix_norm[i])
        hs = rmsnorm(xs, mix_norm[i])
        a = i // N_MIXERS
        if i % N_MIXERS == 0:
            prm = (s5_lambda_re[a], s5_lambda_im[a], s5_log_dt[a], s5_b_re[a], s5_b_im[a],
                   s5_c_re[a], s5_c_im[a], s5_d[a], s5_w_glu[a], s5_b_glu[a])
            zeros = jnp.zeros((BATCH, S5_GROUPS, S5_STATE), jnp.float32)
            mp, hr_p, hi_p = s5_mixer(hp, zeros, zeros, *prm)
            ms, hr_s, hi_s = s5_mixer(hs, state_s5_re[a], state_s5_im[a], *prm)
            s5_re_p.append(hr_p)
            s5_im_p.append(hi_p)
            s5_re_s.append(hr_s)
            s5_im_s.append(hi_s)
        else:
            lam_init = 0.8 - 0.6 * math.exp(-0.3 * i)
            lam = (jnp.exp(jnp.sum(attn_lambda_q1[a].astype(jnp.float32) * attn_lambda_k1[a].astype(jnp.float32)))
                   - jnp.exp(jnp.sum(attn_lambda_q2[a].astype(jnp.float32) * attn_lambda_k2[a].astype(jnp.float32)))
                   + lam_init)
            qp, kp, vp = diff_attn_project(hp, pos_p, attn_w_qkv[a], attn_q_norm[a], attn_k_norm[a])
            qs, kn, vn = diff_attn_project(hs, pos_s, attn_w_qkv[a], attn_q_norm[a], attn_k_norm[a])
            op = diff_attn_prompt(qp, kp, vp)
            os_ = diff_attn_sample(qs, kn, vn, cache_k, cache_v, a, page_table)
            mp = diff_attn_merge(op, lam, lam_init, attn_subln[a], attn_w_o[a], xp.dtype)
            ms = diff_attn_merge(os_, lam, lam_init, attn_subln[a], attn_w_o[a], xs.dtype)
            k_p.append(kp)
            v_p.append(vp)
            k_s.append(kn)
            v_s.append(vn)
        xp = xp + mp
        xs = xs + ms
        xp = xp + 0.5 * swiglu(rmsnorm(xp, ffn_norm[i, 1]), ffn_w_in[i, 1], ffn_w_out[i, 1])
        xs = xs + 0.5 * swiglu(rmsnorm(xs, ffn_norm[i, 1]), ffn_w_in[i, 1], ffn_w_out[i, 1])
    return (xp, xs,
            jnp.stack(s5_re_p), jnp.stack(s5_im_p), jnp.stack(s5_re_s), jnp.stack(s5_im_s),
            jnp.stack(k_p), jnp.stack(v_p), jnp.stack(k_s), jnp.stack(v_s))
```

```python
import functools
import math

import jax
import jax.numpy as jnp
from jax import lax
from jax.experimental import pallas as pl
from jax.experimental.pallas import tpu as pltpu

F32 = jnp.float32
BF16 = jnp.bfloat16

D_MODEL = 1024
D_FF = 2816
S5_GROUP = 16
S5_GROUPS = D_MODEL // S5_GROUP
S5_STATE = 64
S5_CHUNK = 16
N_HEADS = 8
N_SUB = 2 * N_HEADS
HEAD_DIM = D_MODEL // N_SUB
V_DIM = 2 * HEAD_DIM
ATTN_SCALE = HEAD_DIM ** -0.5
ROPE_THETA = 10000.0
EPS = 1e-6
NEG_INF = -1e30
N_MIXERS = 2

LANES = 128
SUBLANES = 8
MXU_TILE = 256
FF_CHUNK = MXU_TILE
ROW_TILE = 512
ATTN_TILE = 512
PAGES_PER_STEP = 8
S5_GROUPS_PER_STEP = 4
VMEM_LIMIT = 48 * 1024 * 1024


def _cparams(*sem):
    return pltpu.CompilerParams(dimension_semantics=sem, vmem_limit_bytes=VMEM_LIMIT)


def _resident(shape):
    zeros = (0,) * len(shape)
    return pl.BlockSpec(shape, lambda *_: zeros, pipeline_mode=pl.Buffered(1))


def _rms(x, g):
    return x * lax.rsqrt(jnp.mean(x * x, axis=-1, keepdims=True) + EPS) * g


def _dot(a, b):
    return jnp.dot(a, b, preferred_element_type=F32)


def _dot_nt(a, b):
    return lax.dot_general(a, b, (((1,), (1,)), ((), ())), preferred_element_type=F32)


def _row_tile(rows):
    return ROW_TILE if rows % ROW_TILE == 0 else rows


def _ffn_body(x_ref, g_ref, win_ref, wout_ref, *rest, norm_dtype):
    if norm_dtype is None:
        o_ref, act_ref = rest
    else:
        gm_ref, o_ref, hn_ref, act_ref = rest
    x = x_ref[...]
    xn = _rms(x, g_ref[...]).astype(BF16)
    for c in range(D_FF // FF_CHUNK):
        lo = c * FF_CHUNK
        g = _dot(xn, win_ref[:, lo:lo + FF_CHUNK])
        u = _dot(xn, win_ref[:, D_FF + lo:D_FF + lo + FF_CHUNK])
        act_ref[:, lo:lo + FF_CHUNK] = (g * jax.nn.sigmoid(g) * u).astype(BF16)
    xo = x + 0.5 * _dot(act_ref[...], wout_ref[...])
    o_ref[...] = xo
    if norm_dtype is not None:
        hn_ref[...] = _rms(xo, gm_ref[...]).astype(norm_dtype)


def _ffn(x, g, w_in, w_out, g_mix=None, norm_dtype=None):
    rows = x.shape[0]
    tm = _row_tile(rows)
    row_spec = pl.BlockSpec((tm, D_MODEL), lambda i: (i, 0))
    in_specs = [row_spec, _resident((1, D_MODEL)), _resident((D_MODEL, 2 * D_FF)), _resident((D_FF, D_MODEL))]
    args = [x, g.reshape(1, D_MODEL), w_in, w_out]
    out_shape = [jax.ShapeDtypeStruct((rows, D_MODEL), F32)]
    out_specs = [row_spec]
    if norm_dtype is not None:
        in_specs.append(_resident((1, D_MODEL)))
        args.append(g_mix.reshape(1, D_MODEL))
        out_shape.append(jax.ShapeDtypeStruct((rows, D_MODEL), norm_dtype))
        out_specs.append(row_spec)
    res = pl.pallas_call(
        functools.partial(_ffn_body, norm_dtype=norm_dtype),
        grid=(rows // tm,),
        in_specs=in_specs,
        out_specs=out_specs,
        out_shape=out_shape,
        scratch_shapes=[pltpu.VMEM((tm, D_FF), BF16)],
        compiler_params=_cparams("parallel"),
        name="ffn_norm" if norm_dtype is not None else "ffn",
    )(*args)
    return res if norm_dtype is not None else res[0]


def _s5_discretise(lam_re, lam_im, log_dt, b_re, b_im):
    dt = jnp.exp(log_dt.astype(F32))[:, None]
    lr = lam_re.astype(F32)
    li = lam_im.astype(F32)
    mag = jnp.exp(lr * dt)
    a_re = mag * jnp.cos(li * dt)
    a_im = mag * jnp.sin(li * dt)
    den = lr * lr + li * li
    nr = a_re - 1.0
    z_re = (nr * lr + a_im * li) / den
    z_im = (a_im * lr - nr * li) / den
    br = b_re.astype(F32)
    bi = b_im.astype(F32)
    bb_re = z_re[..., None] * br - z_im[..., None] * bi
    bb_im = z_re[..., None] * bi + z_im[..., None] * br
    return a_re, a_im, bb_re, bb_im, lr * dt, li * dt


def _s5_chunk_tables(lam_re, lam_im, log_dt, b_re, b_im, c_re, c_im):
    L, G, N, C = S5_CHUNK, S5_GROUPS, S5_STATE, S5_GROUP
    _, _, bb_re, bb_im, ldr, ldi = _s5_discretise(lam_re, lam_im, log_dt, b_re, b_im)
    tau = jnp.arange(L + 1, dtype=F32)[:, None, None]
    mag = jnp.exp(ldr[None] * tau)
    p_re = mag * jnp.cos(ldi[None] * tau)
    p_im = mag * jnp.sin(ldi[None] * tau)
    cr = c_re.astype(F32)
    ci = c_im.astype(F32)
    hi = lax.Precision.HIGHEST
    ca_re = cr[None] * p_re[:, :, None, :] - ci[None] * p_im[:, :, None, :]
    ca_im = cr[None] * p_im[:, :, None, :] + ci[None] * p_re[:, :, None, :]
    resp = (jnp.einsum('tgon,gni->gtio', ca_re[:L], bb_re, precision=hi)
            - jnp.einsum('tgon,gni->gtio', ca_im[:L], bb_im, precision=hi))
    s_idx = jnp.arange(L)[:, None]
    t_idx = jnp.arange(L)[None, :]
    lag = jnp.clip(t_idx - s_idx, 0, L - 1)
    toep = jnp.where((t_idx >= s_idx)[None, :, :, None, None], resp[:, lag], 0.0)
    toep = toep.transpose(0, 1, 3, 2, 4).reshape(G, L * C, L * C)
    pw_re = p_re[L - 1 - jnp.arange(L)]
    pw_im = p_im[L - 1 - jnp.arange(L)]
    w_re = pw_re[..., None] * bb_re[None] - pw_im[..., None] * bb_im[None]
    w_im = pw_re[..., None] * bb_im[None] + pw_im[..., None] * bb_re[None]
    w_re = w_re.transpose(1, 0, 3, 2).reshape(G, L * C, N)
    w_im = w_im.transpose(1, 0, 3, 2).reshape(G, L * C, N)
    w_in = jnp.concatenate([w_re, w_im, w_im, w_re], axis=-1)
    v_re = ca_re[1:].transpose(1, 3, 0, 2).reshape(G, N, L * C)
    v_im = -ca_im[1:].transpose(1, 3, 0, 2).reshape(G, N, L * C)
    v_out = jnp.concatenate([v_re, v_im], axis=1)
    al_re, al_im = p_re[L], p_im[L]
    decay = jnp.stack([jnp.concatenate([al_re, al_re], -1),
                       jnp.concatenate([-al_im, al_im], -1),
                       jnp.concatenate([al_im, -al_im], -1)], axis=1)
    return toep.astype(BF16), w_in.astype(BF16), v_out.astype(BF16), decay


def _s5_prompt_body(u_ref, toep_ref, win_ref, vout_ref, dec_ref, y_ref, hfin_ref, s_ref, hp_ref, *, bsz):
    gps = u_ref.shape[0]
    rows = u_ref.shape[1]
    n2 = 2 * S5_STATE
    cpv = SUBLANES // bsz
    for gi in range(gps):
        s_ref[gi] = _dot(u_ref[gi], win_ref[gi])
    decs = [dec_ref[gi] for gi in range(gps)]

    def step(i, carry):
        r0 = pl.multiple_of(i * SUBLANES, SUBLANES)
        new = []
        for gi in range(gps):
            h, hs = carry[gi]
            a1 = decs[gi][0:1, :]
            a2 = decs[gi][1:2, :]
            a2s = decs[gi][2:3, :]
            slab = s_ref[gi, pl.ds(r0, SUBLANES), :]
            prev = []
            for c in range(cpv):
                prev.append(h)
                sc = slab[c * bsz:(c + 1) * bsz, :]
                h, hs = (a1 * h + a2 * hs + sc[:, :n2], a1 * hs + a2s * h + sc[:, n2:])
            hp_ref[gi, pl.ds(r0, SUBLANES), :] = jnp.concatenate(prev, axis=0)
            new.append((h, hs))
        return tuple(new)

    zero = jnp.zeros((bsz, n2), F32)
    fin = lax.fori_loop(0, rows // SUBLANES, step, tuple((zero, zero) for _ in range(gps)))
    for gi in range(gps):
        hfin_ref[gi] = fin[gi][0]
        y_ref[gi] = _dot(u_ref[gi], toep_ref[gi]) + _dot(hp_ref[gi].astype(BF16), vout_ref[gi])


def _s5_prompt(hn, tables, bsz):
    toep, w_in, v_out, decay = tables
    G, C, L, N = S5_GROUPS, S5_GROUP, S5_CHUNK, S5_STATE
    t = hn.shape[0] // bsz
    nch = t // L
    rows = nch * bsz
    gps = S5_GROUPS_PER_STEP
    lc = L * C
    u = hn.reshape(bsz, nch, L, G, C).transpose(3, 1, 0, 2, 4).reshape(G, rows, lc)
    gspec = lambda *shape: pl.BlockSpec((gps,) + shape, lambda i: (i,) + (0,) * len(shape))
    y, hfin = pl.pallas_call(
        functools.partial(_s5_prompt_body, bsz=bsz),
        grid=(G // gps,),
        in_specs=[gspec(rows, lc), gspec(lc, lc), gspec(lc, lc), gspec(2 * N, lc), gspec(3, 2 * N)],
        out_specs=[gspec(rows, lc), gspec(bsz, 2 * N)],
        out_shape=[jax.ShapeDtypeStruct((G, rows, lc), F32), jax.ShapeDtypeStruct((G, bsz, 2 * N), F32)],
        scratch_shapes=[pltpu.VMEM((gps, rows, lc), F32), pltpu.VMEM((gps, rows, 2 * N), F32)],
        compiler_params=_cparams("parallel"),
        name="s5_prompt",
    )(u, toep, w_in, v_out, decay)
    y = y.reshape(G, nch, bsz, L, C).transpose(2, 1, 3, 0, 4).reshape(bsz * t, D_MODEL)
    h_re = hfin[:, :, :N].transpose(1, 0, 2)
    h_im = hfin[:, :, N:].transpose(1, 0, 2)
    return y, h_re, h_im


def _s5_sample_body(u_ref, bbr_ref, bbi_ref, ar_ref, ai_ref, h0r_ref, h0i_ref, cr_ref, ci_ref,
                    y_ref, hr_ref, hi_ref):
    hi = lax.Precision.HIGHEST
    u = u_ref[...]
    bu_re = jnp.einsum('gbc,gnc->gbn', u, bbr_ref[...], precision=hi, preferred_element_type=F32)
    bu_im = jnp.einsum('gbc,gnc->gbn', u, bbi_ref[...], precision=hi, preferred_element_type=F32)
    ar, ai = ar_ref[...], ai_ref[...]
    h0r, h0i = h0r_ref[...], h0i_ref[...]
    hr = bu_re + ar * h0r - ai * h0i
    hi_ = bu_im + ar * h0i + ai * h0r
    hr_ref[...] = hr
    hi_ref[...] = hi_
    y_ref[...] = (jnp.einsum('gbn,gcn->gbc', hr, cr_ref[...], precision=hi, preferred_element_type=F32)
                  - jnp.einsum('gbn,gcn->gbc', hi_, ci_ref[...], precision=hi, preferred_element_type=F32))


def _s5_sample(hs, h0_re, h0_im, lam_re, lam_im, log_dt, b_re, b_im, c_re, c_im):
    G, C, N = S5_GROUPS, S5_GROUP, S5_STATE
    b = hs.shape[0]
    a_re, a_im, bb_re, bb_im, _, _ = _s5_discretise(lam_re, lam_im, log_dt, b_re, b_im)
    u = hs.reshape(b, G, C).transpose(1, 0, 2)
    args = [u, bb_re, bb_im, a_re[:, None, :], a_im[:, None, :],
            h0_re.astype(F32).transpose(1, 0, 2), h0_im.astype(F32).transpose(1, 0, 2),
            c_re.astype(F32), c_im.astype(F32)]
    full = lambda a: pl.BlockSpec(a.shape, lambda i: (0,) * a.ndim)
    outs = [jax.ShapeDtypeStruct((G, b, C), F32), jax.ShapeDtypeStruct((G, b, N), F32),
            jax.ShapeDtypeStruct((G, b, N), F32)]
    y, hr, hi = pl.pallas_call(
        _s5_sample_body,
        grid=(1,),
        in_specs=[full(a) for a in args],
        out_specs=[full(o) for o in outs],
        out_shape=outs,
        compiler_params=_cparams("arbitrary"),
        name="s5_sample",
    )(*args)
    return y.transpose(1, 0, 2).reshape(b, D_MODEL), hr.transpose(1, 0, 2), hi.transpose(1, 0, 2)


def _s5_post_body(x_ref, y_ref, gm_ref, d_ref, w_ref, b_ref, o_ref):
    x = x_ref[...]
    hp = _rms(x, gm_ref[...])
    z = jax.nn.gelu(y_ref[...] + d_ref[...] * hp)
    zz = _dot(z.astype(BF16), w_ref[...]) + b_ref[...]
    o_ref[...] = x + zz[:, :D_MODEL] * jax.nn.sigmoid(zz[:, D_MODEL:])


def _s5_post(x, y, g_mix, d, w_glu, b_glu):
    rows = x.shape[0]
    tm = _row_tile(rows)
    row_spec = pl.BlockSpec((tm, D_MODEL), lambda i: (i, 0))
    return pl.pallas_call(
        _s5_post_body,
        grid=(rows // tm,),
        in_specs=[row_spec, row_spec, _resident((1, D_MODEL)), _resident((1, D_MODEL)),
                  _resident((D_MODEL, 2 * D_MODEL)), _resident((1, 2 * D_MODEL))],
        out_specs=row_spec,
        out_shape=jax.ShapeDtypeStruct((rows, D_MODEL), F32),
        compiler_params=_cparams("parallel"),
        name="s5_post",
    )(x, y, g_mix.reshape(1, D_MODEL), d.astype(F32).reshape(1, D_MODEL), w_glu, b_glu.astype(F32).reshape(1, -1))


def _rope_tables(pos):
    half = HEAD_DIM // 2
    inv = ROPE_THETA ** (-jnp.arange(half, dtype=F32) / half)
    ang = pos.astype(F32)[:, None] * inv[None, :]
    cos = jnp.cos(ang)
    sin = jnp.sin(ang)
    cos_t = jnp.concatenate([cos, cos, cos, cos], axis=-1)
    sin_t = jnp.concatenate([-sin, sin, -sin, sin], axis=-1)
    return cos_t, sin_t


def _head_sum_matrix():
    head = jnp.arange(D_MODEL) // HEAD_DIM
    return (head[:, None] == head[None, :]).astype(BF16)


def _qkv_body(hn_ref, w_ref, qg_ref, kg_ref, hsum_ref, cos_ref, sin_ref,
              q_ref, kf_ref, kb_ref, vf_ref, vb_ref):
    hn = hn_ref[...]
    cos = cos_ref[...]
    sin = sin_ref[...]
    lane = lax.broadcasted_iota(jnp.int32, cos.shape, 1)
    low_half = (lane % HEAD_DIM) < (HEAD_DIM // 2)

    def norm_rope(x, g):
        ss = _dot((x * x).astype(BF16), hsum_ref[...])
        y = x * lax.rsqrt(ss * (1.0 / HEAD_DIM) + EPS) * g
        outs = []
        for j in range(D_MODEL // LANES):
            t = y[:, j * LANES:(j + 1) * LANES]
            below = pltpu.roll(t, HEAD_DIM // 2, 1)
            above = pltpu.roll(t, LANES - HEAD_DIM // 2, 1)
            outs.append(t * cos + jnp.where(low_half, above, below) * sin)
        return jnp.concatenate(outs, axis=1)

    q = norm_rope(_dot(hn, w_ref[:, 0:D_MODEL]), qg_ref[...])
    q_ref[...] = (q * ATTN_SCALE).astype(BF16)
    k = norm_rope(_dot(hn, w_ref[:, D_MODEL:2 * D_MODEL]), kg_ref[...])
    kf_ref[...] = k
    kb_ref[...] = k.astype(BF16)
    v = _dot(hn, w_ref[:, 2 * D_MODEL:3 * D_MODEL])
    vf_ref[...] = v
    vb_ref[...] = v.astype(BF16)


def _qkv(hn, w_qkv, q_norm, k_norm, cos_t, sin_t):
    rows = hn.shape[0]
    tm = _row_tile(rows)
    tab_blocks = cos_t.shape[0] // tm
    row_spec = pl.BlockSpec((tm, D_MODEL), lambda i: (i, 0))
    tab_spec = pl.BlockSpec((tm, LANES), lambda i: (i % tab_blocks, 0))
    gain = lambda g: jnp.tile(g.astype(F32), N_SUB).reshape(1, D_MODEL)
    outs = [jax.ShapeDtypeStruct((rows, D_MODEL), dt) for dt in (BF16, F32, BF16, F32, BF16)]
    return pl.pallas_call(
        _qkv_body,
        grid=(rows // tm,),
        in_specs=[row_spec, _resident((D_MODEL, 3 * D_MODEL)), _resident((1, D_MODEL)), _resident((1, D_MODEL)),
                  _resident((D_MODEL, D_MODEL)), tab_spec, tab_spec],
        out_specs=[row_spec] * 5,
        out_shape=outs,
        compiler_params=_cparams("parallel"),
        name="qkv",
    )(hn, w_qkv, gain(q_norm), gain(k_norm), _head_sum_matrix(), cos_t, sin_t)


def _lambda_full(lq1_ref, lk1_ref, lq2_ref, lk2_ref, lam_init):
    s1 = jnp.sum(lq1_ref[...] * lk1_ref[...], axis=-1, keepdims=True)
    s2 = jnp.sum(lq2_ref[...] * lk2_ref[...], axis=-1, keepdims=True)
    return jnp.exp(s1) - jnp.exp(s2) + lam_init


def _merge(o1, o2, lam, subln, lam_init):
    out = o1 - lam * o2
    return _rms(out, subln) * (1.0 - lam_init)


def _attn_prompt_body(q_ref, k_ref, v_ref, lq1_ref, lk1_ref, lq2_ref, lk2_ref, subln_ref, o_ref, *, lam_init):
    tq = q_ref.shape[0]
    tk = tq
    qi = pl.program_id(2)
    q = q_ref[...]
    lane = lax.broadcasted_iota(jnp.int32, q.shape, 1)
    zero = jnp.zeros_like(q)
    qs = jnp.concatenate([jnp.where(lane < HEAD_DIM, q, zero), jnp.where(lane >= HEAD_DIM, q, zero)], axis=0)

    def update(j, carry, diagonal):
        m, l, acc = carry
        k0 = pl.multiple_of(j * tk, tk)
        kj = k_ref[pl.ds(k0, tk), :]
        vj = v_ref[pl.ds(k0, tk), :]
        s = _dot_nt(qs, kj)
        if diagonal:
            row = lax.broadcasted_iota(jnp.int32, s.shape, 0) % tq
            col = lax.broadcasted_iota(jnp.int32, s.shape, 1)
            s = jnp.where(col <= row, s, NEG_INF)
        m_new = jnp.maximum(m, jnp.max(s, axis=-1, keepdims=True))
        alpha = jnp.exp(m - m_new)
        p = jnp.exp(s - m_new)
        l_new = l * alpha + jnp.sum(p, axis=-1, keepdims=True)
        acc_new = acc * alpha + _dot(p.astype(BF16), vj)
        return m_new, l_new, acc_new

    init = (jnp.full((2 * tq, 1), NEG_INF, F32), jnp.zeros((2 * tq, 1), F32), jnp.zeros((2 * tq, V_DIM), F32))
    carry = lax.fori_loop(0, qi, functools.partial(update, diagonal=False), init)
    _, l, acc = update(qi, carry, True)
    o = acc / l
    lam = _lambda_full(lq1_ref, lk1_ref, lq2_ref, lk2_ref, lam_init)
    o_ref[...] = _merge(o[:tq], o[tq:], lam, subln_ref[...], lam_init).astype(BF16)


def _attn_prompt(q, k, v, lams, subln, lam_init, bsz):
    rows = q.shape[0]
    t = rows // bsz
    tq = ATTN_TILE if t % ATTN_TILE == 0 else t
    nq = t // tq
    small = pl.BlockSpec((1, HEAD_DIM), lambda b, h, i: (0, 0))
    return pl.pallas_call(
        functools.partial(_attn_prompt_body, lam_init=lam_init),
        grid=(bsz, N_HEADS, nq),
        in_specs=[pl.BlockSpec((tq, V_DIM), lambda b, h, i: (b * nq + i, h)),
                  pl.BlockSpec((t, V_DIM), lambda b, h, i: (b, h)),
                  pl.BlockSpec((t, V_DIM), lambda b, h, i: (b, h)),
                  small, small, small, small,
                  pl.BlockSpec((1, V_DIM), lambda b, h, i: (0, 0))],
        out_specs=pl.BlockSpec((tq, V_DIM), lambda b, h, i: (b * nq + i, h)),
        out_shape=jax.ShapeDtypeStruct((rows, D_MODEL), BF16),
        compiler_params=_cparams("parallel", "parallel", "arbitrary"),
        name="attn_prompt",
    )(q, k, v, *[a.astype(F32).reshape(1, HEAD_DIM) for a in lams], subln.astype(F32).reshape(1, V_DIM))


def _attn_pages_body(pt_ref, qr_ref, *refs):
    del pt_ref
    npg = PAGES_PER_STEP
    k_refs, v_refs = refs[:npg], refs[npg:2 * npg]
    m_ref, l_ref, acc_ref = refs[2 * npg:]
    g = pl.program_id(1)

    @pl.when(g == 0)
    def _():
        m_ref[...] = jnp.full(m_ref.shape, NEG_INF, F32)
        l_ref[...] = jnp.zeros(l_ref.shape, F32)
        acc_ref[...] = jnp.zeros(acc_ref.shape, F32)

    qr = qr_ref[0]
    s = jnp.concatenate([_dot_nt(qr, k_refs[i][0, 0].astype(BF16)) for i in range(npg)], axis=1)
    m_old = m_ref[0][:, 0:1]
    l_old = l_ref[0][:, 0:1]
    m_new = jnp.maximum(m_old, jnp.max(s, axis=-1, keepdims=True))
    alpha = jnp.exp(m_old - m_new)
    p = jnp.exp(s - m_new)
    l_new = l_old * alpha + jnp.sum(p, axis=-1, keepdims=True)
    pb = p.astype(BF16)
    page = k_refs[0].shape[2]
    pv = _dot(pb[:, 0:page], v_refs[0][0, 0].astype(BF16))
    for i in range(1, npg):
        pv = pv + _dot(pb[:, i * page:(i + 1) * page], v_refs[i][0, 0].astype(BF16))
    acc_ref[0] = acc_ref[0] * alpha + pv
    m_ref[0] = jnp.broadcast_to(m_new, m_ref.shape[1:])
    l_ref[0] = jnp.broadcast_to(l_new, l_ref.shape[1:])


def _attn_pages(q_rows, cache_k, cache_v, layer, page_table):
    b, n_pages = page_table.shape
    n_layers, n_phys, page = cache_k.shape[:3]
    ck = cache_k.reshape(n_layers, n_phys, page, D_MODEL)
    cv = cache_v.reshape(n_layers, n_phys, page, D_MODEL)
    npg = PAGES_PER_STEP
    assert n_pages % npg == 0

    def page_spec(i):
        return pl.BlockSpec((1, 1, page, D_MODEL), lambda s, g, pt: (layer, pt[s, g * npg + i], 0, 0))

    stat_spec = pl.BlockSpec((1, N_SUB, LANES), lambda s, g, pt: (s, 0, 0))
    acc_spec = pl.BlockSpec((1, N_SUB, D_MODEL), lambda s, g, pt: (s, 0, 0))
    grid_spec = pltpu.PrefetchScalarGridSpec(
        num_scalar_prefetch=1,
        grid=(b, n_pages // npg),
        in_specs=[acc_spec] + [page_spec(i) for i in range(npg)] * 2,
        out_specs=[stat_spec, stat_spec, acc_spec],
    )
    return pl.pallas_call(
        _attn_pages_body,
        grid_spec=grid_spec,
        out_shape=[jax.ShapeDtypeStruct((b, N_SUB, LANES), F32), jax.ShapeDtypeStruct((b, N_SUB, LANES), F32),
                   jax.ShapeDtypeStruct((b, N_SUB, D_MODEL), F32)],
        compiler_params=_cparams("parallel", "arbitrary"),
        name="attn_pages",
    )(page_table, q_rows, *([ck] * npg), *([cv] * npg))


def _attn_sample_finish_body(qr_ref, kn_ref, vn_ref, m_ref, l_ref, acc_ref,
                             lq1_ref, lk1_ref, lq2_ref, lk2_ref, subln_ref, o_ref, *, lam_init):
    qr = qr_ref[0].astype(F32)
    kn = kn_ref[0].astype(F32)
    vn = vn_ref[0].astype(F32)
    s = jnp.sum(qr * kn, axis=-1, keepdims=True)
    m_old = m_ref[0][:, 0:1]
    l_old = l_ref[0][:, 0:1]
    m_new = jnp.maximum(m_old, s)
    alpha = jnp.exp(m_old - m_new)
    p = jnp.exp(s - m_new)
    l_new = l_old * alpha + p
    acc = acc_ref[0] * alpha + p.astype(BF16).astype(F32) * vn
    o = acc / l_new
    lam = _lambda_full(lq1_ref, lk1_ref, lq2_ref, lk2_ref, lam_init)
    heads = []
    for h in range(N_HEADS):
        lanes = slice(h * V_DIM, (h + 1) * V_DIM)
        heads.append(_merge(o[2 * h:2 * h + 1, lanes], o[2 * h + 1:2 * h + 2, lanes], lam, subln_ref[...], lam_init))
    o_ref[0] = jnp.concatenate(heads, axis=1).astype(BF16)


def _attn_sample_finish(q_rows, k_new, v_new, m, l, acc, lams, subln, lam_init):
    b = q_rows.shape[0]
    seq3 = lambda w: pl.BlockSpec((1, 1, w), lambda s: (s, 0, 0))
    sub3 = lambda w: pl.BlockSpec((1, N_SUB, w), lambda s: (s, 0, 0))
    small = pl.BlockSpec((1, HEAD_DIM), lambda s: (0, 0))
    out = pl.pallas_call(
        functools.partial(_attn_sample_finish_body, lam_init=lam_init),
        grid=(b,),
        in_specs=[sub3(D_MODEL), seq3(D_MODEL), seq3(D_MODEL), sub3(LANES), sub3(LANES), sub3(D_MODEL),
                  small, small, small, small, pl.BlockSpec((1, V_DIM), lambda s: (0, 0))],
        out_specs=seq3(D_MODEL),
        out_shape=jax.ShapeDtypeStruct((b, 1, D_MODEL), BF16),
        compiler_params=_cparams("parallel"),
        name="attn_sample_finish",
    )(q_rows, k_new.reshape(b, 1, D_MODEL), v_new.reshape(b, 1, D_MODEL), m, l, acc,
      *[a.astype(F32).reshape(1, HEAD_DIM) for a in lams], subln.astype(F32).reshape(1, V_DIM))
    return out.reshape(b, D_MODEL)


def _out_proj_body(x_ref, a_ref, w_ref, o_ref):
    o_ref[...] = x_ref[...] + _dot(a_ref[...], w_ref[...])


def _out_proj(x, a, w_o):
    rows = x.shape[0]
    tm = _row_tile(rows)
    row_spec = pl.BlockSpec((tm, D_MODEL), lambda i: (i, 0))
    return pl.pallas_call(
        _out_proj_body,
        grid=(rows // tm,),
        in_specs=[row_spec, row_spec, _resident((D_MODEL, D_MODEL))],
        out_specs=row_spec,
        out_shape=jax.ShapeDtypeStruct((rows, D_MODEL), F32),
        compiler_params=_cparams("parallel"),
        name="out_proj",
    )(x, a, w_o)


def _block_diag_rows(q):
    head = jnp.arange(D_MODEL) // HEAD_DIM
    keep = head[None, :] == jnp.arange(N_SUB)[:, None]
    return jnp.where(keep[None], q[:, None, :], jnp.zeros((), q.dtype))


def kernel(x_prompt, x_sample, state_s5_re, state_s5_im, cache_k, cache_v, page_table, ffn_norm, ffn_w_in, ffn_w_out, mix_norm, s5_lambda_re, s5_lambda_im, s5_log_dt, s5_b_re, s5_b_im, s5_c_re, s5_c_im, s5_d, s5_w_glu, s5_b_glu, attn_w_qkv, attn_q_norm, attn_k_norm, attn_lambda_q1, attn_lambda_k1, attn_lambda_q2, attn_lambda_k2, attn_subln, attn_w_o):
    bsz, seq, _ = x_prompt.shape
    dbs, dseq, _ = x_sample.shape
    assert dseq == 1, "the sample group decodes one token per sequence"
    assert SUBLANES % bsz == 0 and seq % S5_CHUNK == 0
    depth = ffn_norm.shape[0]
    past_len = page_table.shape[1] * cache_k.shape[2]

    xp = x_prompt.reshape(bsz * seq, D_MODEL)
    xs = x_sample.reshape(dbs * dseq, D_MODEL)
    w_in = ffn_w_in.astype(BF16)
    w_out = ffn_w_out.astype(BF16)
    f32n = ffn_norm.astype(F32)
    mixn = mix_norm.astype(F32)

    s5_re_p, s5_im_p, s5_re_s, s5_im_s = [], [], [], []
    k_p, v_p, k_s, v_s = [], [], [], []
    for i in range(depth):
        a = i // N_MIXERS
        if i % N_MIXERS == 0:
            xp, hp = _ffn(xp, f32n[i, 0], w_in[i, 0], w_out[i, 0], mixn[i], BF16)
            xs, hs = _ffn(xs, f32n[i, 0], w_in[i, 0], w_out[i, 0], mixn[i], F32)
            prm = (s5_lambda_re[a], s5_lambda_im[a], s5_log_dt[a], s5_b_re[a], s5_b_im[a], s5_c_re[a], s5_c_im[a])
            w_glu = s5_w_glu[a].astype(BF16)
            yp, hr_p, hi_p = _s5_prompt(hp, _s5_chunk_tables(*prm), bsz)
            ys, hr_s, hi_s = _s5_sample(hs, state_s5_re[a], state_s5_im[a], *prm)
            xp = _s5_post(xp, yp, mixn[i], s5_d[a], w_glu, s5_b_glu[a])
            xs = _s5_post(xs, ys, mixn[i], s5_d[a], w_glu, s5_b_glu[a])
            s5_re_p.append(hr_p)
            s5_im_p.append(hi_p)
            s5_re_s.append(hr_s)
            s5_im_s.append(hi_s)
        else:
            xp, hp = _ffn(xp, f32n[i, 0], w_in[i, 0], w_out[i, 0], mixn[i], BF16)
            xs, hs = _ffn(xs, f32n[i, 0], w_in[i, 0], w_out[i, 0], mixn[i], BF16)
            lam_init = 0.8 - 0.6 * math.exp(-0.3 * i)
            lams = (attn_lambda_q1[a], attn_lambda_k1[a], attn_lambda_q2[a], attn_lambda_k2[a])
            w_qkv = attn_w_qkv[a].astype(BF16)
            w_o = attn_w_o[a].astype(BF16)
            cos_p, sin_p = _rope_tables(jnp.arange(seq))
            cos_s, sin_s = _rope_tables(jnp.tile(past_len + jnp.arange(dseq), dbs))
            qp, kpf, kpb, vpf, vpb = _qkv(hp, w_qkv, attn_q_norm[a], attn_k_norm[a], cos_p, sin_p)
            qs, ksf, ksb, vsf, vsb = _qkv(hs, w_qkv, attn_q_norm[a], attn_k_norm[a], cos_s, sin_s)
            op = _attn_prompt(qp, kpb, vpb, lams, attn_subln[a], lam_init, bsz)
            q_rows = _block_diag_rows(qs)
            m, l, acc = _attn_pages(q_rows, cache_k, cache_v, a, page_table)
            os_ = _attn_sample_finish(q_rows, ksb, vsb, m, l, acc, lams, attn_subln[a], lam_init)
            xp = _out_proj(xp, op, w_o)
            xs = _out_proj(xs, os_, w_o)
            k_p.append(kpf.reshape(bsz, seq, N_SUB, HEAD_DIM))
            v_p.append(vpf.reshape(bsz, seq, N_HEADS, V_DIM))
            k_s.append(ksf.reshape(dbs, dseq, N_SUB, HEAD_DIM))
            v_s.append(vsf.reshape(dbs, dseq, N_HEADS, V_DIM))
        xp = _ffn(xp, f32n[i, 1], w_in[i, 1], w_out[i, 1])
        xs = _ffn(xs, f32n[i, 1], w_in[i, 1], w_out[i, 1])
    return (xp.reshape(bsz, seq, D_MODEL), xs.reshape(dbs, dseq, D_MODEL),
            jnp.stack(s5_re_p), jnp.stack(s5_im_p), jnp.stack(s5_re_s), jnp.stack(s5_im_s),
            jnp.stack(k_p), jnp.stack(v_p), jnp.stack(k_s), jnp.stack(v_s))
```

```python
import functools
import math

import jax
import jax.numpy as jnp
from jax import lax
from jax.experimental import pallas as pl
from jax.experimental.pallas import tpu as pltpu

F32 = jnp.float32
BF16 = jnp.bfloat16

D_MODEL = 1024
D_FF = 2816
S5_GROUP = 16
S5_GROUPS = D_MODEL // S5_GROUP
S5_STATE = 64
S5_CHUNK = 16
N_HEADS = 8
N_SUB = 2 * N_HEADS
HEAD_DIM = D_MODEL // N_SUB
V_DIM = 2 * HEAD_DIM
ATTN_SCALE = HEAD_DIM ** -0.5
ROPE_THETA = 10000.0
EPS = 1e-6
NEG_INF = -1e30
N_MIXERS = 2

LANES = 128
SUBLANES = 8
MXU_TILE = 256
FF_CHUNK = MXU_TILE
ROW_TILE = 512
ATTN_TILE = 512
PAGES_PER_STEP = 8
S5_GROUPS_PER_STEP = 4
VMEM_LIMIT = 48 * 1024 * 1024


def _cparams(*sem):
    return pltpu.CompilerParams(dimension_semantics=sem, vmem_limit_bytes=VMEM_LIMIT)


def _resident(shape):
    zeros = (0,) * len(shape)
    return pl.BlockSpec(shape, lambda *_: zeros, pipeline_mode=pl.Buffered(1))


def _rms(x, g):
    return x * lax.rsqrt(jnp.mean(x * x, axis=-1, keepdims=True) + EPS) * g


def _dot(a, b):
    return jnp.dot(a, b, preferred_element_type=F32)


def _dot_nt(a, b):
    return lax.dot_general(a, b, (((1,), (1,)), ((), ())), preferred_element_type=F32)


def _row_tile(rows):
    return ROW_TILE if rows % ROW_TILE == 0 else rows


def _ffn_body(x_ref, g_ref, win_ref, wout_ref, *rest, norm_dtype):
    if norm_dtype is None:
        o_ref, act_ref = rest
    else:
        gm_ref, o_ref, hn_ref, act_ref = rest
    x = x_ref[...]
    xn = _rms(x, g_ref[...]).astype(BF16)
    for c in range(D_FF // FF_CHUNK):
        lo = c * FF_CHUNK
        g = _dot(xn, win_ref[:, lo:lo + FF_CHUNK])
        u = _dot(xn, win_ref[:, D_FF + lo:D_FF + lo + FF_CHUNK])
        act_ref[:, lo:lo + FF_CHUNK] = (g * jax.nn.sigmoid(g) * u).astype(BF16)
    xo = x + 0.5 * _dot(act_ref[...], wout_ref[...])
    o_ref[...] = xo
    if norm_dtype is not None:
        hn_ref[...] = _rms(xo, gm_ref[...]).astype(norm_dtype)


def _ffn(x, g, w_in, w_out, g_mix=None, norm_dtype=None):
    rows = x.shape[0]
    tm = _row_tile(rows)
    row_spec = pl.BlockSpec((tm, D_MODEL), lambda i: (i, 0))
    in_specs = [row_spec, _resident((1, D_MODEL)), _resident((D_MODEL, 2 * D_FF)), _resident((D_FF, D_MODEL))]
    args = [x, g.reshape(1, D_MODEL), w_in, w_out]
    out_shape = [jax.ShapeDtypeStruct((rows, D_MODEL), F32)]
    out_specs = [row_spec]
    if norm_dtype is not None:
        in_specs.append(_resident((1, D_MODEL)))
        args.append(g_mix.reshape(1, D_MODEL))
        out_shape.append(jax.ShapeDtypeStruct((rows, D_MODEL), norm_dtype))
        out_specs.append(row_spec)
    res = pl.pallas_call(
        functools.partial(_ffn_body, norm_dtype=norm_dtype),
        grid=(rows // tm,),
        in_specs=in_specs,
        out_specs=out_specs,
        out_shape=out_shape,
        scratch_shapes=[pltpu.VMEM((tm, D_FF), BF16)],
        compiler_params=_cparams("parallel"),
        name="ffn_norm" if norm_dtype is not None else "ffn",
    )(*args)
    return res if norm_dtype is not None else res[0]


def _s5_discretise(lam_re, lam_im, log_dt, b_re, b_im):
    dt = jnp.exp(log_dt.astype(F32))[:, None]
    lr = lam_re.astype(F32)
    li = lam_im.astype(F32)
    mag = jnp.exp(lr * dt)
    a_re = mag * jnp.cos(li * dt)
    a_im = mag * jnp.sin(li * dt)
    den = lr * lr + li * li
    nr = a_re - 1.0
    z_re = (nr * lr + a_im * li) / den
    z_im = (a_im * lr - nr * li) / den
    br = b_re.astype(F32)
    bi = b_im.astype(F32)
    bb_re = z_re[..., None] * br - z_im[..., None] * bi
    bb_im = z_re[..., None] * bi + z_im[..., None] * br
    return a_re, a_im, bb_re, bb_im, lr * dt, li * dt


def _s5_chunk_tables(lam_re, lam_im, log_dt, b_re, b_im, c_re, c_im):
    L, G, N, C = S5_CHUNK, S5_GROUPS, S5_STATE, S5_GROUP
    _, _, bb_re, bb_im, ldr, ldi = _s5_discretise(lam_re, lam_im, log_dt, b_re, b_im)
    tau = jnp.arange(L + 1, dtype=F32)[:, None, None]
    mag = jnp.exp(ldr[None] * tau)
    p_re = mag * jnp.cos(ldi[None] * tau)
    p_im = mag * jnp.sin(ldi[None] * tau)
    cr = c_re.astype(F32)
    ci = c_im.astype(F32)
    hi = lax.Precision.HIGHEST
    ca_re = cr[None] * p_re[:, :, None, :] - ci[None] * p_im[:, :, None, :]
    ca_im = cr[None] * p_im[:, :, None, :] + ci[None] * p_re[:, :, None, :]
    resp = (jnp.einsum('tgon,gni->gtio', ca_re[:L], bb_re, precision=hi)
            - jnp.einsum('tgon,gni->gtio', ca_im[:L], bb_im, precision=hi))
    s_idx = jnp.arange(L)[:, None]
    t_idx = jnp.arange(L)[None, :]
    lag = jnp.clip(t_idx - s_idx, 0, L - 1)
    toep = jnp.where((t_idx >= s_idx)[None, :, :, None, None], resp[:, lag], 0.0)
    toep = toep.transpose(0, 1, 3, 2, 4).reshape(G, L * C, L * C)
    pw_re = p_re[L - 1 - jnp.arange(L)]
    pw_im = p_im[L - 1 - jnp.arange(L)]
    w_re = pw_re[..., None] * bb_re[None] - pw_im[..., None] * bb_im[None]
    w_im = pw_re[..., None] * bb_im[None] + pw_im[..., None] * bb_re[None]
    w_re = w_re.transpose(1, 0, 3, 2).reshape(G, L * C, N)
    w_im = w_im.transpose(1, 0, 3, 2).reshape(G, L * C, N)
    w_in = jnp.concatenate([w_re, w_im, w_im, w_re], axis=-1)
    v_re = ca_re[1:].transpose(1, 3, 0, 2).reshape(G, N, L * C)
    v_im = -ca_im[1:].transpose(1, 3, 0, 2).reshape(G, N, L * C)
    v_out = jnp.concatenate([v_re, v_im], axis=1)
    al_re, al_im = p_re[L], p_im[L]
    decay = jnp.stack([jnp.concatenate([al_re, al_re], -1),
                       jnp.concatenate([-al_im, al_im], -1),
                       jnp.concatenate([al_im, -al_im], -1)], axis=1)
    return toep.astype(BF16), w_in.astype(BF16), v_out.astype(BF16), decay


def _s5_prompt_body(u_ref, toep_ref, win_ref, vout_ref, dec_ref, y_ref, hfin_ref, s_ref, hp_ref, *, bsz):
    gps = u_ref.shape[0]
    rows = u_ref.shape[1]
    n2 = 2 * S5_STATE
    cpv = SUBLANES // bsz
    for gi in range(gps):
        s_ref[gi] = _dot(u_ref[gi], win_ref[gi])
    decs = [dec_ref[gi] for gi in range(gps)]

    def step(i, carry):
        r0 = pl.multiple_of(i * SUBLANES, SUBLANES)
        new = []
        for gi in range(gps):
            h, hs = carry[gi]
            a1 = decs[gi][0:1, :]
            a2 = decs[gi][1:2, :]
            a2s = decs[gi][2:3, :]
            slab = s_ref[gi, pl.ds(r0, SUBLANES), :]
            prev = []
            for c in range(cpv):
                prev.append(h)
                sc = slab[c * bsz:(c + 1) * bsz, :]
                h, hs = (a1 * h + a2 * hs + sc[:, :n2], a1 * hs + a2s * h + sc[:, n2:])
            hp_ref[gi, pl.ds(r0, SUBLANES), :] = jnp.concatenate(prev, axis=0)
            new.append((h, hs))
        return tuple(new)

    zero = jnp.zeros((bsz, n2), F32)
    fin = lax.fori_loop(0, rows // SUBLANES, step, tuple((zero, zero) for _ in range(gps)))
    for gi in range(gps):
        hfin_ref[gi] = fin[gi][0]
        y_ref[gi] = _dot(u_ref[gi], toep_ref[gi]) + _dot(hp_ref[gi].astype(BF16), vout_ref[gi])


def _s5_prompt(hn, tables, bsz):
    toep, w_in, v_out, decay = tables
    G, C, L, N = S5_GROUPS, S5_GROUP, S5_CHUNK, S5_STATE
    t = hn.shape[0] // bsz
    nch = t // L
    rows = nch * bsz
    gps = S5_GROUPS_PER_STEP
    lc = L * C
    u = hn.reshape(bsz, nch, L, G, C).transpose(3, 1, 0, 2, 4).reshape(G, rows, lc)
    gspec = lambda *shape: pl.BlockSpec((gps,) + shape, lambda i: (i,) + (0,) * len(shape))
    y, hfin = pl.pallas_call(
        functools.partial(_s5_prompt_body, bsz=bsz),
        grid=(G // gps,),
        in_specs=[gspec(rows, lc), gspec(lc, lc), gspec(lc, lc), gspec(2 * N, lc), gspec(3, 2 * N)],
        out_specs=[gspec(rows, lc), gspec(bsz, 2 * N)],
        out_shape=[jax.ShapeDtypeStruct((G, rows, lc), F32), jax.ShapeDtypeStruct((G, bsz, 2 * N), F32)],
        scratch_shapes=[pltpu.VMEM((gps, rows, lc), F32), pltpu.VMEM((gps, rows, 2 * N), F32)],
        compiler_params=_cparams("parallel"),
        name="s5_prompt",
    )(u, toep, w_in, v_out, decay)
    y = y.reshape(G, nch, bsz, L, C).transpose(2, 1, 3, 0, 4).reshape(bsz * t, D_MODEL)
    h_re = hfin[:, :, :N].transpose(1, 0, 2)
    h_im = hfin[:, :, N:].transpose(1, 0, 2)
    return y, h_re, h_im


def _s5_sample_body(u_ref, bbr_ref, bbi_ref, ar_ref, ai_ref, h0r_ref, h0i_ref, cr_ref, ci_ref,
                    y_ref, hr_ref, hi_ref):
    hi = lax.Precision.HIGHEST
    u = u_ref[...]
    bu_re = jnp.einsum('gbc,gnc->gbn', u, bbr_ref[...], precision=hi, preferred_element_type=F32)
    bu_im = jnp.einsum('gbc,gnc->gbn', u, bbi_ref[...], precision=hi, preferred_element_type=F32)
    ar, ai = ar_ref[...], ai_ref[...]
    h0r, h0i = h0r_ref[...], h0i_ref[...]
    hr = bu_re + ar * h0r - ai * h0i
    hi_ = bu_im + ar * h0i + ai * h0r
    hr_ref[...] = hr
    hi_ref[...] = hi_
    y_ref[...] = (jnp.einsum('gbn,gcn->gbc', hr, cr_ref[...], precision=hi, preferred_element_type=F32)
                  - jnp.einsum('gbn,gcn->gbc', hi_, ci_ref[...], precision=hi, preferred_element_type=F32))


def _s5_sample(hs, h0_re, h0_im, lam_re, lam_im, log_dt, b_re, b_im, c_re, c_im):
    G, C, N = S5_GROUPS, S5_GROUP, S5_STATE
    b = hs.shape[0]
    a_re, a_im, bb_re, bb_im, _, _ = _s5_discretise(lam_re, lam_im, log_dt, b_re, b_im)
    u = hs.reshape(b, G, C).transpose(1, 0, 2)
    args = [u, bb_re, bb_im, a_re[:, None, :], a_im[:, None, :],
            h0_re.astype(F32).transpose(1, 0, 2), h0_im.astype(F32).transpose(1, 0, 2),
            c_re.astype(F32), c_im.astype(F32)]
    full = lambda a: pl.BlockSpec(a.shape, lambda i: (0,) * a.ndim)
    outs = [jax.ShapeDtypeStruct((G, b, C), F32), jax.ShapeDtypeStruct((G, b, N), F32),
            jax.ShapeDtypeStruct((G, b, N), F32)]
    y, hr, hi = pl.pallas_call(
        _s5_sample_body,
        grid=(1,),
        in_specs=[full(a) for a in args],
        out_specs=[full(o) for o in outs],
        out_shape=outs,
        compiler_params=_cparams("arbitrary"),
        name="s5_sample",
    )(*args)
    return y.transpose(1, 0, 2).reshape(b, D_MODEL), hr.transpose(1, 0, 2), hi.transpose(1, 0, 2)


def _s5_post_body(x_ref, y_ref, gm_ref, d_ref, w_ref, b_ref, o_ref):
    x = x_ref[...]
    hp = _rms(x, gm_ref[...])
    z = jax.nn.gelu(y_ref[...] + d_ref[...] * hp)
    zz = _dot(z.astype(BF16), w_ref[...]) + b_ref[...]
    o_ref[...] = x + zz[:, :D_MODEL] * jax.nn.sigmoid(zz[:, D_MODEL:])


def _s5_post(x, y, g_mix, d, w_glu, b_glu):
    rows = x.shape[0]
    tm = _row_tile(rows)
    row_spec = pl.BlockSpec((tm, D_MODEL), lambda i: (i, 0))
    return pl.pallas_call(
        _s5_post_body,
        grid=(rows // tm,),
        in_specs=[row_spec, row_spec, _resident((1, D_MODEL)), _resident((1, D_MODEL)),
                  _resident((D_MODEL, 2 * D_MODEL)), _resident((1, 2 * D_MODEL))],
        out_specs=row_spec,
        out_shape=jax.ShapeDtypeStruct((rows, D_MODEL), F32),
        compiler_params=_cparams("parallel"),
        name="s5_post",
    )(x, y, g_mix.reshape(1, D_MODEL), d.astype(F32).reshape(1, D_MODEL), w_glu, b_glu.astype(F32).reshape(1, -1))


def _rope_tables(pos):
    half = HEAD_DIM // 2
    inv = ROPE_THETA ** (-jnp.arange(half, dtype=F32) / half)
    ang = pos.astype(F32)[:, None] * inv[None, :]
    cos = jnp.cos(ang)
    sin = jnp.sin(ang)
    cos_t = jnp.concatenate([cos, cos, cos, cos], axis=-1)
    sin_t = jnp.concatenate([-sin, sin, -sin, sin], axis=-1)
    return cos_t, sin_t


def _head_sum_matrix():
    head = jnp.arange(D_MODEL) // HEAD_DIM
    return (head[:, None] == head[None, :]).astype(BF16)


def _qkv_body(hn_ref, w_ref, qg_ref, kg_ref, hsum_ref, cos_ref, sin_ref,
              q_ref, kf_ref, kb_ref, vf_ref, vb_ref, *, seq_minor_k):
    hn = hn_ref[...]
    cos = cos_ref[...]
    sin = sin_ref[...]
    lane = lax.broadcasted_iota(jnp.int32, cos.shape, 1)
    low_half = (lane % HEAD_DIM) < (HEAD_DIM // 2)

    def norm_rope(x, g):
        ss = _dot((x * x).astype(BF16), hsum_ref[...])
        y = x * lax.rsqrt(ss * (1.0 / HEAD_DIM) + EPS) * g
        outs = []
        for j in range(D_MODEL // LANES):
            t = y[:, j * LANES:(j + 1) * LANES]
            below = pltpu.roll(t, HEAD_DIM // 2, 1)
            above = pltpu.roll(t, LANES - HEAD_DIM // 2, 1)
            outs.append(t * cos + jnp.where(low_half, above, below) * sin)
        return jnp.concatenate(outs, axis=1)

    q = norm_rope(_dot(hn, w_ref[:, 0:D_MODEL]), qg_ref[...])
    q_ref[...] = (q * ATTN_SCALE).astype(BF16)
    k = norm_rope(_dot(hn, w_ref[:, D_MODEL:2 * D_MODEL]), kg_ref[...])
    if seq_minor_k:
        kf_ref[0] = k.T.reshape(kf_ref.shape[1:])
    else:
        kf_ref[...] = k
    kb_ref[...] = k.astype(BF16)
    v = _dot(hn, w_ref[:, 2 * D_MODEL:3 * D_MODEL])
    vf_ref[...] = v
    vb_ref[...] = v.astype(BF16)


def _qkv(hn, w_qkv, q_norm, k_norm, cos_t, sin_t, seq_minor_k=False):
    rows = hn.shape[0]
    tm = _row_tile(rows)
    tab_blocks = cos_t.shape[0] // tm
    row_spec = pl.BlockSpec((tm, D_MODEL), lambda i: (i, 0))
    tab_spec = pl.BlockSpec((tm, LANES), lambda i: (i % tab_blocks, 0))
    gain = lambda g: jnp.tile(g.astype(F32), N_SUB).reshape(1, D_MODEL)
    outs = [jax.ShapeDtypeStruct((rows, D_MODEL), dt) for dt in (BF16, F32, BF16, F32, BF16)]
    out_specs = [row_spec] * 5
    if seq_minor_k:
        period = cos_t.shape[0]
        outs[1] = jax.ShapeDtypeStruct((rows // period, N_SUB, HEAD_DIM, period), F32)
        out_specs[1] = pl.BlockSpec((1, N_SUB, HEAD_DIM, tm), lambda i: (i // tab_blocks, 0, 0, i % tab_blocks))
    return pl.pallas_call(
        functools.partial(_qkv_body, seq_minor_k=seq_minor_k),
        grid=(rows // tm,),
        in_specs=[row_spec, _resident((D_MODEL, 3 * D_MODEL)), _resident((1, D_MODEL)), _resident((1, D_MODEL)),
                  _resident((D_MODEL, D_MODEL)), tab_spec, tab_spec],
        out_specs=out_specs,
        out_shape=outs,
        compiler_params=_cparams("parallel"),
        name="qkv",
    )(hn, w_qkv, gain(q_norm), gain(k_norm), _head_sum_matrix(), cos_t, sin_t)


def _lambda_full(lq1_ref, lk1_ref, lq2_ref, lk2_ref, lam_init):
    s1 = jnp.sum(lq1_ref[...] * lk1_ref[...], axis=-1, keepdims=True)
    s2 = jnp.sum(lq2_ref[...] * lk2_ref[...], axis=-1, keepdims=True)
    return jnp.exp(s1) - jnp.exp(s2) + lam_init


def _merge(o1, o2, lam, subln, lam_init):
    out = o1 - lam * o2
    return _rms(out, subln) * (1.0 - lam_init)


def _attn_prompt_body(q_ref, k_ref, v_ref, lq1_ref, lk1_ref, lq2_ref, lk2_ref, subln_ref, o_ref, *, lam_init):
    tq = q_ref.shape[0]
    tk = tq
    qi = pl.program_id(2)
    q = q_ref[...]
    lane = lax.broadcasted_iota(jnp.int32, q.shape, 1)
    zero = jnp.zeros_like(q)
    qs = jnp.concatenate([jnp.where(lane < HEAD_DIM, q, zero), jnp.where(lane >= HEAD_DIM, q, zero)], axis=0)

    def update(j, carry, diagonal):
        m, l, acc = carry
        k0 = pl.multiple_of(j * tk, tk)
        kj = k_ref[pl.ds(k0, tk), :]
        vj = v_ref[pl.ds(k0, tk), :]
        s = _dot_nt(qs, kj)
        if diagonal:
            row = lax.broadcasted_iota(jnp.int32, s.shape, 0) % tq
            col = lax.broadcasted_iota(jnp.int32, s.shape, 1)
            s = jnp.where(col <= row, s, NEG_INF)
        m_new = jnp.maximum(m, jnp.max(s, axis=-1, keepdims=True))
        alpha = jnp.exp(m - m_new)
        p = jnp.exp(s - m_new)
        l_new = l * alpha + jnp.sum(p, axis=-1, keepdims=True)
        acc_new = acc * alpha + _dot(p.astype(BF16), vj)
        return m_new, l_new, acc_new

    init = (jnp.full((2 * tq, 1), NEG_INF, F32), jnp.zeros((2 * tq, 1), F32), jnp.zeros((2 * tq, V_DIM), F32))
    carry = lax.fori_loop(0, qi, functools.partial(update, diagonal=False), init)
    _, l, acc = update(qi, carry, True)
    o = acc / l
    lam = _lambda_full(lq1_ref, lk1_ref, lq2_ref, lk2_ref, lam_init)
    o_ref[...] = _merge(o[:tq], o[tq:], lam, subln_ref[...], lam_init).astype(BF16)


def _attn_prompt(q, k, v, lams, subln, lam_init, bsz):
    rows = q.shape[0]
    t = rows // bsz
    tq = ATTN_TILE if t % ATTN_TILE == 0 else t
    nq = t // tq
    small = pl.BlockSpec((1, HEAD_DIM), lambda b, h, i: (0, 0))
    return pl.pallas_call(
        functools.partial(_attn_prompt_body, lam_init=lam_init),
        grid=(bsz, N_HEADS, nq),
        in_specs=[pl.BlockSpec((tq, V_DIM), lambda b, h, i: (b * nq + i, h)),
                  pl.BlockSpec((t, V_DIM), lambda b, h, i: (b, h)),
                  pl.BlockSpec((t, V_DIM), lambda b, h, i: (b, h)),
                  small, small, small, small,
                  pl.BlockSpec((1, V_DIM), lambda b, h, i: (0, 0))],
        out_specs=pl.BlockSpec((tq, V_DIM), lambda b, h, i: (b * nq + i, h)),
        out_shape=jax.ShapeDtypeStruct((rows, D_MODEL), BF16),
        compiler_params=_cparams("parallel", "parallel", "arbitrary"),
        name="attn_prompt",
    )(q, k, v, *[a.astype(F32).reshape(1, HEAD_DIM) for a in lams], subln.astype(F32).reshape(1, V_DIM))


def _attn_pages_body(pt_ref, qr_ref, *refs):
    del pt_ref
    npg = PAGES_PER_STEP
    k_refs, v_refs = refs[:npg], refs[npg:2 * npg]
    m_ref, l_ref, acc_ref = refs[2 * npg:]
    g = pl.program_id(1)

    @pl.when(g == 0)
    def _():
        m_ref[...] = jnp.full(m_ref.shape, NEG_INF, F32)
        l_ref[...] = jnp.zeros(l_ref.shape, F32)
        acc_ref[...] = jnp.zeros(acc_ref.shape, F32)

    qr = qr_ref[0]
    page = k_refs[0].shape[-1]

    def keys_t(i):
        return k_refs[i][0, 0].reshape(D_MODEL, page).astype(BF16)

    def values(i):
        heads = [v_refs[i][0, 0, pl.ds(h, page, stride=N_HEADS), :] for h in range(N_HEADS)]
        return jnp.concatenate(heads, axis=1).astype(BF16)

    s = jnp.concatenate([_dot(qr, keys_t(i)) for i in range(npg)], axis=1)
    m_old = m_ref[0][:, 0:1]
    l_old = l_ref[0][:, 0:1]
    m_new = jnp.maximum(m_old, jnp.max(s, axis=-1, keepdims=True))
    alpha = jnp.exp(m_old - m_new)
    p = jnp.exp(s - m_new)
    l_new = l_old * alpha + jnp.sum(p, axis=-1, keepdims=True)
    pb = p.astype(BF16)
    pv = _dot(pb[:, 0:page], values(0))
    for i in range(1, npg):
        pv = pv + _dot(pb[:, i * page:(i + 1) * page], values(i))
    acc_ref[0] = acc_ref[0] * alpha + pv
    m_ref[0] = jnp.broadcast_to(m_new, m_ref.shape[1:])
    l_ref[0] = jnp.broadcast_to(l_new, l_ref.shape[1:])


def _attn_pages(q_rows, cache_k, cache_v, layer, page_table):
    b, n_pages = page_table.shape
    page = cache_k.shape[2]
    ck = jnp.transpose(cache_k, (0, 1, 3, 4, 2))
    npg = PAGES_PER_STEP
    assert n_pages % npg == 0

    def k_spec(i):
        return pl.BlockSpec((1, 1, N_SUB, HEAD_DIM, page), lambda s, g, pt: (layer, pt[s, g * npg + i], 0, 0, 0))

    cv = cache_v.reshape(cache_v.shape[:2] + (page * N_HEADS, V_DIM))

    def v_spec(i):
        return pl.BlockSpec((1, 1, page * N_HEADS, V_DIM), lambda s, g, pt: (layer, pt[s, g * npg + i], 0, 0))

    stat_spec = pl.BlockSpec((1, N_SUB, LANES), lambda s, g, pt: (s, 0, 0))
    acc_spec = pl.BlockSpec((1, N_SUB, D_MODEL), lambda s, g, pt: (s, 0, 0))
    grid_spec = pltpu.PrefetchScalarGridSpec(
        num_scalar_prefetch=1,
        grid=(b, n_pages // npg),
        in_specs=[acc_spec] + [k_spec(i) for i in range(npg)] + [v_spec(i) for i in range(npg)],
        out_specs=[stat_spec, stat_spec, acc_spec],
    )
    return pl.pallas_call(
        _attn_pages_body,
        grid_spec=grid_spec,
        out_shape=[jax.ShapeDtypeStruct((b, N_SUB, LANES), F32), jax.ShapeDtypeStruct((b, N_SUB, LANES), F32),
                   jax.ShapeDtypeStruct((b, N_SUB, D_MODEL), F32)],
        compiler_params=_cparams("parallel", "arbitrary"),
        name="attn_pages",
    )(page_table, q_rows, *([ck] * npg), *([cv] * npg))


def _attn_sample_finish_body(qr_ref, kn_ref, vn_ref, m_ref, l_ref, acc_ref,
                             lq1_ref, lk1_ref, lq2_ref, lk2_ref, subln_ref, o_ref, *, lam_init):
    qr = qr_ref[0].astype(F32)
    kn = kn_ref[0].astype(F32)
    vn = vn_ref[0].astype(F32)
    s = jnp.sum(qr * kn, axis=-1, keepdims=True)
    m_old = m_ref[0][:, 0:1]
    l_old = l_ref[0][:, 0:1]
    m_new = jnp.maximum(m_old, s)
    alpha = jnp.exp(m_old - m_new)
    p = jnp.exp(s - m_new)
    l_new = l_old * alpha + p
    acc = acc_ref[0] * alpha + p.astype(BF16).astype(F32) * vn
    o = acc / l_new
    lam = _lambda_full(lq1_ref, lk1_ref, lq2_ref, lk2_ref, lam_init)
    heads = []
    for h in range(N_HEADS):
        lanes = slice(h * V_DIM, (h + 1) * V_DIM)
        heads.append(_merge(o[2 * h:2 * h + 1, lanes], o[2 * h + 1:2 * h + 2, lanes], lam, subln_ref[...], lam_init))
    o_ref[0] = jnp.concatenate(heads, axis=1).astype(BF16)


def _attn_sample_finish(q_rows, k_new, v_new, m, l, acc, lams, subln, lam_init):
    b = q_rows.shape[0]
    seq3 = lambda w: pl.BlockSpec((1, 1, w), lambda s: (s, 0, 0))
    sub3 = lambda w: pl.BlockSpec((1, N_SUB, w), lambda s: (s, 0, 0))
    small = pl.BlockSpec((1, HEAD_DIM), lambda s: (0, 0))
    out = pl.pallas_call(
        functools.partial(_attn_sample_finish_body, lam_init=lam_init),
        grid=(b,),
        in_specs=[sub3(D_MODEL), seq3(D_MODEL), seq3(D_MODEL), sub3(LANES), sub3(LANES), sub3(D_MODEL),
                  small, small, small, small, pl.BlockSpec((1, V_DIM), lambda s: (0, 0))],
        out_specs=seq3(D_MODEL),
        out_shape=jax.ShapeDtypeStruct((b, 1, D_MODEL), BF16),
        compiler_params=_cparams("parallel"),
        name="attn_sample_finish",
    )(q_rows, k_new.reshape(b, 1, D_MODEL), v_new.reshape(b, 1, D_MODEL), m, l, acc,
      *[a.astype(F32).reshape(1, HEAD_DIM) for a in lams], subln.astype(F32).reshape(1, V_DIM))
    return out.reshape(b, D_MODEL)


def _out_proj_body(x_ref, a_ref, w_ref, o_ref):
    o_ref[...] = x_ref[...] + _dot(a_ref[...], w_ref[...])


def _out_proj(x, a, w_o):
    rows = x.shape[0]
    tm = _row_tile(rows)
    row_spec = pl.BlockSpec((tm, D_MODEL), lambda i: (i, 0))
    return pl.pallas_call(
        _out_proj_body,
        grid=(rows // tm,),
        in_specs=[row_spec, row_spec, _resident((D_MODEL, D_MODEL))],
        out_specs=row_spec,
        out_shape=jax.ShapeDtypeStruct((rows, D_MODEL), F32),
        compiler_params=_cparams("parallel"),
        name="out_proj",
    )(x, a, w_o)


def _block_diag_rows(q):
    head = jnp.arange(D_MODEL) // HEAD_DIM
    keep = head[None, :] == jnp.arange(N_SUB)[:, None]
    return jnp.where(keep[None], q[:, None, :], jnp.zeros((), q.dtype))


def kernel(x_prompt, x_sample, state_s5_re, state_s5_im, cache_k, cache_v, page_table, ffn_norm, ffn_w_in, ffn_w_out, mix_norm, s5_lambda_re, s5_lambda_im, s5_log_dt, s5_b_re, s5_b_im, s5_c_re, s5_c_im, s5_d, s5_w_glu, s5_b_glu, attn_w_qkv, attn_q_norm, attn_k_norm, attn_lambda_q1, attn_lambda_k1, attn_lambda_q2, attn_lambda_k2, attn_subln, attn_w_o):
    bsz, seq, _ = x_prompt.shape
    dbs, dseq, _ = x_sample.shape
    assert dseq == 1, "the sample group decodes one token per sequence"
    assert SUBLANES % bsz == 0 and seq % S5_CHUNK == 0
    depth = ffn_norm.shape[0]
    past_len = page_table.shape[1] * cache_k.shape[2]

    xp = x_prompt.reshape(bsz * seq, D_MODEL)
    xs = x_sample.reshape(dbs * dseq, D_MODEL)
    w_in = ffn_w_in.astype(BF16)
    w_out = ffn_w_out.astype(BF16)
    f32n = ffn_norm.astype(F32)
    mixn = mix_norm.astype(F32)

    s5_re_p, s5_im_p, s5_re_s, s5_im_s = [], [], [], []
    k_p, v_p, k_s, v_s = [], [], [], []
    for i in range(depth):
        a = i // N_MIXERS
        if i % N_MIXERS == 0:
            xp, hp = _ffn(xp, f32n[i, 0], w_in[i, 0], w_out[i, 0], mixn[i], BF16)
            xs, hs = _ffn(xs, f32n[i, 0], w_in[i, 0], w_out[i, 0], mixn[i], F32)
            prm = (s5_lambda_re[a], s5_lambda_im[a], s5_log_dt[a], s5_b_re[a], s5_b_im[a], s5_c_re[a], s5_c_im[a])
            w_glu = s5_w_glu[a].astype(BF16)
            yp, hr_p, hi_p = _s5_prompt(hp, _s5_chunk_tables(*prm), bsz)
            ys, hr_s, hi_s = _s5_sample(hs, state_s5_re[a], state_s5_im[a], *prm)
            xp = _s5_post(xp, yp, mixn[i], s5_d[a], w_glu, s5_b_glu[a])
            xs = _s5_post(xs, ys, mixn[i], s5_d[a], w_glu, s5_b_glu[a])
            s5_re_p.append(hr_p)
            s5_im_p.append(hi_p)
            s5_re_s.append(hr_s)
            s5_im_s.append(hi_s)
        else:
            xp, hp = _ffn(xp, f32n[i, 0], w_in[i, 0], w_out[i, 0], mixn[i], BF16)
            xs, hs = _ffn(xs, f32n[i, 0], w_in[i, 0], w_out[i, 0], mixn[i], BF16)
            lam_init = 0.8 - 0.6 * math.exp(-0.3 * i)
            lams = (attn_lambda_q1[a], attn_lambda_k1[a], attn_lambda_q2[a], attn_lambda_k2[a])
            w_qkv = attn_w_qkv[a].astype(BF16)
            w_o = attn_w_o[a].astype(BF16)
            cos_p, sin_p = _rope_tables(jnp.arange(seq))
            cos_s, sin_s = _rope_tables(jnp.tile(past_len + jnp.arange(dseq), dbs))
            qp, kpf, kpb, vpf, vpb = _qkv(hp, w_qkv, attn_q_norm[a], attn_k_norm[a], cos_p, sin_p, seq_minor_k=True)
            qs, ksf, ksb, vsf, vsb = _qkv(hs, w_qkv, attn_q_norm[a], attn_k_norm[a], cos_s, sin_s)
            op = _attn_prompt(qp, kpb, vpb, lams, attn_subln[a], lam_init, bsz)
            q_rows = _block_diag_rows(qs)
            m, l, acc = _attn_pages(q_rows, cache_k, cache_v, a, page_table)
            os_ = _attn_sample_finish(q_rows, ksb, vsb, m, l, acc, lams, attn_subln[a], lam_init)
            xp = _out_proj(xp, op, w_o)
            xs = _out_proj(xs, os_, w_o)
            k_p.append(kpf.transpose(0, 3, 1, 2))
            v_p.append(vpf.reshape(bsz, seq, N_HEADS, V_DIM))
            k_s.append(ksf.reshape(dbs, dseq, N_SUB, HEAD_DIM))
            v_s.append(vsf.reshape(dbs, dseq, N_HEADS, V_DIM))
        xp = _ffn(xp, f32n[i, 1], w_in[i, 1], w_out[i, 1])
        xs = _ffn(xs, f32n[i, 1], w_in[i, 1], w_out[i, 1])
    return (xp.reshape(bsz, seq, D_MODEL), xs.reshape(dbs, dseq, D_MODEL),
            jnp.stack(s5_re_p), jnp.stack(s5_im_p), jnp.stack(s5_re_s), jnp.stack(s5_im_s),
            jnp.stack(k_p), jnp.stack(v_p), jnp.stack(k_s), jnp.stack(v_s))
```

```python
import functools
import math

import jax
import jax.numpy as jnp
from jax import lax
from jax.experimental import pallas as pl
from jax.experimental.pallas import tpu as pltpu

F32 = jnp.float32
BF16 = jnp.bfloat16

D_MODEL = 1024
D_FF = 2816
S5_GROUP = 16
S5_GROUPS = D_MODEL // S5_GROUP
S5_STATE = 64
S5_CHUNK = 16
N_HEADS = 8
N_SUB = 2 * N_HEADS
HEAD_DIM = D_MODEL // N_SUB
V_DIM = 2 * HEAD_DIM
ATTN_SCALE = HEAD_DIM ** -0.5
ROPE_THETA = 10000.0
EPS = 1e-6
NEG_INF = -1e30
N_MIXERS = 2

LANES = 128
SUBLANES = 8
MXU_TILE = 256
FF_CHUNK = MXU_TILE
ROW_TILE = 512
ATTN_Q_TILE = 1024
ATTN_K_TILE = 512
PAGES_PER_STEP = 8
LANE_GROUPS = LANES // S5_GROUP
S5_SEQS_PER_STEP = 2
S5_RELAYOUT_ROWS = 64
VMEM_LIMIT = 48 * 1024 * 1024


def _cparams(*sem):
    return pltpu.CompilerParams(dimension_semantics=sem, vmem_limit_bytes=VMEM_LIMIT)


def _resident(shape):
    zeros = (0,) * len(shape)
    return pl.BlockSpec(shape, lambda *_: zeros, pipeline_mode=pl.Buffered(1))


def _rms(x, g):
    return x * lax.rsqrt(jnp.mean(x * x, axis=-1, keepdims=True) + EPS) * g


def _dot(a, b):
    return jnp.dot(a, b, preferred_element_type=F32)


def _dot_nt(a, b):
    return lax.dot_general(a, b, (((1,), (1,)), ((), ())), preferred_element_type=F32)


def _row_tile(rows):
    return ROW_TILE if rows % ROW_TILE == 0 else rows


def _ffn_body(x_ref, g_ref, win_ref, wout_ref, *rest, norm_dtype):
    if norm_dtype is None:
        o_ref, act_ref = rest
    else:
        gm_ref, o_ref, hn_ref, act_ref = rest
    x = x_ref[...]
    xn = _rms(x, g_ref[...]).astype(BF16)
    for c in range(D_FF // FF_CHUNK):
        lo = c * FF_CHUNK
        g = _dot(xn, win_ref[:, lo:lo + FF_CHUNK])
        u = _dot(xn, win_ref[:, D_FF + lo:D_FF + lo + FF_CHUNK])
        act_ref[:, lo:lo + FF_CHUNK] = (g * jax.nn.sigmoid(g) * u).astype(BF16)
    xo = x + 0.5 * _dot(act_ref[...], wout_ref[...])
    o_ref[...] = xo
    if norm_dtype is not None:
        hn_ref[...] = _rms(xo, gm_ref[...]).astype(norm_dtype)


def _ffn(x, g, w_in, w_out, g_mix=None, norm_dtype=None):
    rows = x.shape[0]
    tm = _row_tile(rows)
    row_spec = pl.BlockSpec((tm, D_MODEL), lambda i: (i, 0))
    in_specs = [row_spec, _resident((1, D_MODEL)), _resident((D_MODEL, 2 * D_FF)), _resident((D_FF, D_MODEL))]
    args = [x, g.reshape(1, D_MODEL), w_in, w_out]
    out_shape = [jax.ShapeDtypeStruct((rows, D_MODEL), F32)]
    out_specs = [row_spec]
    if norm_dtype is not None:
        in_specs.append(_resident((1, D_MODEL)))
        args.append(g_mix.reshape(1, D_MODEL))
        out_shape.append(jax.ShapeDtypeStruct((rows, D_MODEL), norm_dtype))
        out_specs.append(row_spec)
    res = pl.pallas_call(
        functools.partial(_ffn_body, norm_dtype=norm_dtype),
        grid=(rows // tm,),
        in_specs=in_specs,
        out_specs=out_specs,
        out_shape=out_shape,
        scratch_shapes=[pltpu.VMEM((tm, D_FF), BF16)],
        compiler_params=_cparams("parallel"),
        name="ffn_norm" if norm_dtype is not None else "ffn",
    )(*args)
    return res if norm_dtype is not None else res[0]


def _s5_discretise(lam_re, lam_im, log_dt, b_re, b_im):
    dt = jnp.exp(log_dt.astype(F32))[:, None]
    lr = lam_re.astype(F32)
    li = lam_im.astype(F32)
    mag = jnp.exp(lr * dt)
    a_re = mag * jnp.cos(li * dt)
    a_im = mag * jnp.sin(li * dt)
    den = lr * lr + li * li
    nr = a_re - 1.0
    z_re = (nr * lr + a_im * li) / den
    z_im = (a_im * lr - nr * li) / den
    br = b_re.astype(F32)
    bi = b_im.astype(F32)
    bb_re = z_re[..., None] * br - z_im[..., None] * bi
    bb_im = z_re[..., None] * bi + z_im[..., None] * br
    return a_re, a_im, bb_re, bb_im, lr * dt, li * dt


def _s5_chunk_tables(lam_re, lam_im, log_dt, b_re, b_im, c_re, c_im):
    L, G, N, C = S5_CHUNK, S5_GROUPS, S5_STATE, S5_GROUP
    _, _, bb_re, bb_im, ldr, ldi = _s5_discretise(lam_re, lam_im, log_dt, b_re, b_im)
    tau = jnp.arange(L + 1, dtype=F32)[:, None, None]
    mag = jnp.exp(ldr[None] * tau)
    p_re = mag * jnp.cos(ldi[None] * tau)
    p_im = mag * jnp.sin(ldi[None] * tau)
    cr = c_re.astype(F32)
    ci = c_im.astype(F32)
    hi = lax.Precision.HIGHEST
    ca_re = cr[None] * p_re[:, :, None, :] - ci[None] * p_im[:, :, None, :]
    ca_im = cr[None] * p_im[:, :, None, :] + ci[None] * p_re[:, :, None, :]
    resp = (jnp.einsum('tgon,gni->gtio', ca_re[:L], bb_re, precision=hi)
            - jnp.einsum('tgon,gni->gtio', ca_im[:L], bb_im, precision=hi))
    s_idx = jnp.arange(L)[:, None]
    t_idx = jnp.arange(L)[None, :]
    lag = jnp.clip(t_idx - s_idx, 0, L - 1)
    toep = jnp.where((t_idx >= s_idx)[None, :, :, None, None], resp[:, lag], 0.0)
    toep = toep.transpose(0, 1, 3, 2, 4).reshape(G, L * C, L * C)
    pw_re = p_re[L - 1 - jnp.arange(L)]
    pw_im = p_im[L - 1 - jnp.arange(L)]
    w_re = pw_re[..., None] * bb_re[None] - pw_im[..., None] * bb_im[None]
    w_im = pw_re[..., None] * bb_im[None] + pw_im[..., None] * bb_re[None]
    w_re = w_re.transpose(1, 0, 3, 2).reshape(G, L * C, N)
    w_im = w_im.transpose(1, 0, 3, 2).reshape(G, L * C, N)
    w_in = jnp.concatenate([w_re, w_im, w_im, w_re], axis=-1)
    v_re = ca_re[1:].transpose(1, 3, 0, 2).reshape(G, N, L * C)
    v_im = -ca_im[1:].transpose(1, 3, 0, 2).reshape(G, N, L * C)
    v_out = jnp.concatenate([v_re, v_im], axis=1)
    al_re, al_im = p_re[L], p_im[L]
    decay = jnp.stack([jnp.concatenate([al_re, al_re], -1),
                       jnp.concatenate([-al_im, al_im], -1),
                       jnp.concatenate([al_im, -al_im], -1)], axis=1)
    return toep.astype(BF16), w_in.astype(BF16), v_out.astype(BF16), decay


def _block_transpose(xs):
    n = len(xs)
    block = lax.broadcasted_iota(jnp.int32, xs[0].shape, 1) // S5_GROUP
    d = n // 2
    while d >= 1:
        upper = (block // d) % 2 == 1
        nxt = list(xs)
        for i in range(n):
            if i & d == 0:
                a, b = xs[i], xs[i | d]
                nxt[i] = jnp.where(upper, pltpu.roll(b, d * S5_GROUP, 1), a)
                nxt[i | d] = jnp.where(upper, b, pltpu.roll(a, LANES - d * S5_GROUP, 1))
        xs = nxt
        d //= 2
    return xs


def _s5_prompt_body(hn_ref, toep_ref, win_ref, vout_ref, dec_ref, y_ref, hfin_ref, u_scr, sy_scr, hp_scr, *, sps):
    lg = LANE_GROUPS
    rows = u_scr.shape[2]
    nch = rows // sps
    n2 = 2 * S5_STATE
    half = S5_CHUNK // 2
    rb = S5_RELAYOUT_ROWS
    blocks_per_seq = nch // rb

    def chunk_rows(blk):
        seq = blk // blocks_per_seq
        c0 = (blk % blocks_per_seq) * rb
        return c0 * sps + seq, (seq * nch + c0) * S5_CHUNK

    def gather(blk, _):
        r0, t0 = chunk_rows(blk)
        for h in range(2):
            xs = [hn_ref[pl.ds(t0 + h * half + tt, rb, stride=S5_CHUNK), :] for tt in range(half)]
            for g, y in enumerate(_block_transpose(xs)):
                u_scr[g, h, pl.ds(r0, rb, stride=sps), :] = y
        return _

    lax.fori_loop(0, sps * blocks_per_seq, gather, 0)

    def chunk_inputs(g):
        return jnp.concatenate([u_scr[g, 0], u_scr[g, 1]], axis=1).astype(BF16)

    for g in range(lg):
        s = _dot(chunk_inputs(g), win_ref[g])
        sy_scr[g, 0] = s[:, :n2]
        sy_scr[g, 1] = s[:, n2:]
    decs = [dec_ref[g] for g in range(lg)]
    cpv = SUBLANES // sps

    def step(i, carry):
        r0 = pl.multiple_of(i * SUBLANES, SUBLANES)
        new = []
        for g in range(lg):
            h, hs = carry[g]
            a1 = decs[g][0:1, :]
            a2 = decs[g][1:2, :]
            a2s = decs[g][2:3, :]
            slab = sy_scr[g, 0, pl.ds(r0, SUBLANES), :]
            slab_s = sy_scr[g, 1, pl.ds(r0, SUBLANES), :]
            prev = []
            for c in range(cpv):
                prev.append(h)
                rs = slice(c * sps, (c + 1) * sps)
                h, hs = (a1 * h + a2 * hs + slab[rs], a1 * hs + a2s * h + slab_s[rs])
            hp_scr[g, pl.ds(r0, SUBLANES), :] = jnp.concatenate(prev, axis=0)
            new.append((h, hs))
        return tuple(new)

    zero = jnp.zeros((sps, n2), F32)
    fin = lax.fori_loop(0, rows // SUBLANES, step, tuple((zero, zero) for _ in range(lg)))
    for g in range(lg):
        hfin_ref[0, g] = fin[g][0]
        y = _dot(chunk_inputs(g), toep_ref[g]) + _dot(hp_scr[g].astype(BF16), vout_ref[g])
        sy_scr[g, 0] = y[:, :n2]
        sy_scr[g, 1] = y[:, n2:]

    def scatter(blk, _):
        r0, t0 = chunk_rows(blk)
        for h in range(2):
            ys = [sy_scr[g, h, pl.ds(r0, rb, stride=sps), :] for g in range(lg)]
            for tt, x in enumerate(_block_transpose(ys)):
                y_ref[pl.ds(t0 + h * half + tt, rb, stride=S5_CHUNK), :] = x
        return _

    lax.fori_loop(0, sps * blocks_per_seq, scatter, 0)


def _s5_prompt(hn, tables, bsz):
    toep, w_in, v_out, decay = tables
    G, L, N = S5_GROUPS, S5_CHUNK, S5_STATE
    t = hn.shape[0] // bsz
    sps = S5_SEQS_PER_STEP
    assert bsz % sps == 0 and SUBLANES % sps == 0 and (t // L) % S5_RELAYOUT_ROWS == 0
    rows = (t // L) * sps
    lg = LANE_GROUPS
    lc = L * S5_GROUP
    tok_spec = pl.BlockSpec((sps * t, LANES), lambda j, s: (s, j))
    gspec = lambda *shape: pl.BlockSpec((lg,) + shape, lambda j, s: (j,) + (0,) * len(shape))
    y, hfin = pl.pallas_call(
        functools.partial(_s5_prompt_body, sps=sps),
        grid=(G // lg, bsz // sps),
        in_specs=[tok_spec, gspec(lc, lc), gspec(lc, lc), gspec(2 * N, lc), gspec(3, 2 * N)],
        out_specs=[tok_spec, pl.BlockSpec((1, lg, sps, 2 * N), lambda j, s: (s, j, 0, 0))],
        out_shape=[jax.ShapeDtypeStruct(hn.shape, F32), jax.ShapeDtypeStruct((bsz // sps, G, sps, 2 * N), F32)],
        scratch_shapes=[pltpu.VMEM((lg, 2, rows, LANES), F32), pltpu.VMEM((lg, 2, rows, LANES), F32),
                        pltpu.VMEM((lg, rows, 2 * N), F32)],
        compiler_params=_cparams("parallel", "parallel"),
        name="s5_prompt",
    )(hn, toep, w_in, v_out, decay)
    hfin = hfin.transpose(0, 2, 1, 3).reshape(bsz, G, 2 * N)
    return y, hfin[:, :, :N], hfin[:, :, N:]


def _s5_sample_body(u_ref, bbr_ref, bbi_ref, ar_ref, ai_ref, h0r_ref, h0i_ref, cr_ref, ci_ref,
                    y_ref, hr_ref, hi_ref):
    hi = lax.Precision.HIGHEST
    u = u_ref[...]
    bu_re = jnp.einsum('gbc,gnc->gbn', u, bbr_ref[...], precision=hi, preferred_element_type=F32)
    bu_im = jnp.einsum('gbc,gnc->gbn', u, bbi_ref[...], precision=hi, preferred_element_type=F32)
    ar, ai = ar_ref[...], ai_ref[...]
    h0r, h0i = h0r_ref[...], h0i_ref[...]
    hr = bu_re + ar * h0r - ai * h0i
    hi_ = bu_im + ar * h0i + ai * h0r
    hr_ref[...] = hr
    hi_ref[...] = hi_
    y_ref[...] = (jnp.einsum('gbn,gcn->gbc', hr, cr_ref[...], precision=hi, preferred_element_type=F32)
                  - jnp.einsum('gbn,gcn->gbc', hi_, ci_ref[...], precision=hi, preferred_element_type=F32))


def _s5_sample(hs, h0_re, h0_im, lam_re, lam_im, log_dt, b_re, b_im, c_re, c_im):
    G, C, N = S5_GROUPS, S5_GROUP, S5_STATE
    b = hs.shape[0]
    a_re, a_im, bb_re, bb_im, _, _ = _s5_discretise(lam_re, lam_im, log_dt, b_re, b_im)
    u = hs.reshape(b, G, C).transpose(1, 0, 2)
    args = [u, bb_re, bb_im, a_re[:, None, :], a_im[:, None, :],
            h0_re.astype(F32).transpose(1, 0, 2), h0_im.astype(F32).transpose(1, 0, 2),
            c_re.astype(F32), c_im.astype(F32)]
    full = lambda a: pl.BlockSpec(a.shape, lambda i: (0,) * a.ndim)
    outs = [jax.ShapeDtypeStruct((G, b, C), F32), jax.ShapeDtypeStruct((G, b, N), F32),
            jax.ShapeDtypeStruct((G, b, N), F32)]
    y, hr, hi = pl.pallas_call(
        _s5_sample_body,
        grid=(1,),
        in_specs=[full(a) for a in args],
        out_specs=[full(o) for o in outs],
        out_shape=outs,
        compiler_params=_cparams("arbitrary"),
        name="s5_sample",
    )(*args)
    return y.transpose(1, 0, 2).reshape(b, D_MODEL), hr.transpose(1, 0, 2), hi.transpose(1, 0, 2)


def _s5_post_body(x_ref, y_ref, gm_ref, d_ref, w_ref, b_ref, o_ref):
    x = x_ref[...]
    hp = _rms(x, gm_ref[...])
    z = jax.nn.gelu(y_ref[...] + d_ref[...] * hp)
    zz = _dot(z.astype(BF16), w_ref[...]) + b_ref[...]
    o_ref[...] = x + zz[:, :D_MODEL] * jax.nn.sigmoid(zz[:, D_MODEL:])


def _s5_post(x, y, g_mix, d, w_glu, b_glu):
    rows = x.shape[0]
    tm = _row_tile(rows)
    row_spec = pl.BlockSpec((tm, D_MODEL), lambda i: (i, 0))
    return pl.pallas_call(
        _s5_post_body,
        grid=(rows // tm,),
        in_specs=[row_spec, row_spec, _resident((1, D_MODEL)), _resident((1, D_MODEL)),
                  _resident((D_MODEL, 2 * D_MODEL)), _resident((1, 2 * D_MODEL))],
        out_specs=row_spec,
        out_shape=jax.ShapeDtypeStruct((rows, D_MODEL), F32),
        compiler_params=_cparams("parallel"),
        name="s5_post",
    )(x, y, g_mix.reshape(1, D_MODEL), d.astype(F32).reshape(1, D_MODEL), w_glu, b_glu.astype(F32).reshape(1, -1))


def _rope_tables(pos):
    half = HEAD_DIM // 2
    inv = ROPE_THETA ** (-jnp.arange(half, dtype=F32) / half)
    ang = pos.astype(F32)[:, None] * inv[None, :]
    cos = jnp.cos(ang)
    sin = jnp.sin(ang)
    cos_t = jnp.concatenate([cos, cos, cos, cos], axis=-1)
    sin_t = jnp.concatenate([-sin, sin, -sin, sin], axis=-1)
    return cos_t, sin_t


def _head_sum_matrix():
    head = jnp.arange(D_MODEL) // HEAD_DIM
    return (head[:, None] == head[None, :]).astype(BF16)


def _qkv_body(hn_ref, w_ref, qg_ref, kg_ref, hsum_ref, cos_ref, sin_ref,
              q_ref, kf_ref, kb_ref, vf_ref, vb_ref, *, seq_minor_k):
    hn = hn_ref[...]
    cos = cos_ref[...]
    sin = sin_ref[...]
    lane = lax.broadcasted_iota(jnp.int32, cos.shape, 1)
    low_half = (lane % HEAD_DIM) < (HEAD_DIM // 2)

    def norm_rope(x, g):
        ss = _dot((x * x).astype(BF16), hsum_ref[...])
        y = x * lax.rsqrt(ss * (1.0 / HEAD_DIM) + EPS) * g
        outs = []
        for j in range(D_MODEL // LANES):
            t = y[:, j * LANES:(j + 1) * LANES]
            below = pltpu.roll(t, HEAD_DIM // 2, 1)
            above = pltpu.roll(t, LANES - HEAD_DIM // 2, 1)
            outs.append(t * cos + jnp.where(low_half, above, below) * sin)
        return jnp.concatenate(outs, axis=1)

    q = norm_rope(_dot(hn, w_ref[:, 0:D_MODEL]), qg_ref[...])
    q_ref[...] = (q * ATTN_SCALE).astype(BF16)
    k = norm_rope(_dot(hn, w_ref[:, D_MODEL:2 * D_MODEL]), kg_ref[...])
    if seq_minor_k:
        kf_ref[0] = k.T.reshape(kf_ref.shape[1:])
    else:
        kf_ref[...] = k
    kb_ref[...] = k.astype(BF16)
    v = _dot(hn, w_ref[:, 2 * D_MODEL:3 * D_MODEL])
    vf_ref[...] = v
    vb_ref[...] = v.astype(BF16)


def _qkv(hn, w_qkv, q_norm, k_norm, cos_t, sin_t, seq_minor_k=False):
    rows = hn.shape[0]
    tm = _row_tile(rows)
    tab_blocks = cos_t.shape[0] // tm
    row_spec = pl.BlockSpec((tm, D_MODEL), lambda i: (i, 0))
    tab_spec = pl.BlockSpec((tm, LANES), lambda i: (i % tab_blocks, 0))
    gain = lambda g: jnp.tile(g.astype(F32), N_SUB).reshape(1, D_MODEL)
    outs = [jax.ShapeDtypeStruct((rows, D_MODEL), dt) for dt in (BF16, F32, BF16, F32, BF16)]
    out_specs = [row_spec] * 5
    if seq_minor_k:
        period = cos_t.shape[0]
        outs[1] = jax.ShapeDtypeStruct((rows // period, N_SUB, HEAD_DIM, period), F32)
        out_specs[1] = pl.BlockSpec((1, N_SUB, HEAD_DIM, tm), lambda i: (i // tab_blocks, 0, 0, i % tab_blocks))
    return pl.pallas_call(
        functools.partial(_qkv_body, seq_minor_k=seq_minor_k),
        grid=(rows // tm,),
        in_specs=[row_spec, _resident((D_MODEL, 3 * D_MODEL)), _resident((1, D_MODEL)), _resident((1, D_MODEL)),
                  _resident((D_MODEL, D_MODEL)), tab_spec, tab_spec],
        out_specs=out_specs,
        out_shape=outs,
        compiler_params=_cparams("parallel"),
        name="qkv",
    )(hn, w_qkv, gain(q_norm), gain(k_norm), _head_sum_matrix(), cos_t, sin_t)


def _lambda_full(lq1_ref, lk1_ref, lq2_ref, lk2_ref, lam_init):
    s1 = jnp.sum(lq1_ref[...] * lk1_ref[...], axis=-1, keepdims=True)
    s2 = jnp.sum(lq2_ref[...] * lk2_ref[...], axis=-1, keepdims=True)
    return jnp.exp(s1) - jnp.exp(s2) + lam_init


def _merge(o1, o2, lam, subln, lam_init):
    out = o1 - lam * o2
    return _rms(out, subln) * (1.0 - lam_init)


def _attn_prompt_body(q_ref, k_ref, v_ref, lq1_ref, lk1_ref, lq2_ref, lk2_ref, subln_ref, o_ref, *, lam_init, tk):
    tq = q_ref.shape[0]
    qi = pl.program_id(2)
    kt_per_q = tq // tk
    qt = q_ref[...].astype(F32).T.astype(BF16)
    dim = lax.broadcasted_iota(jnp.int32, qt.shape, 0)
    zero = jnp.zeros_like(qt)
    qst = jnp.concatenate([jnp.where(dim < HEAD_DIM, qt, zero), jnp.where(dim >= HEAD_DIM, qt, zero)], axis=1)

    span = MXU_TILE
    n_span = 2 * tq // span

    def update(j, carry, diagonal=None):
        k0 = pl.multiple_of(j * tk, tk)
        kj = k_ref[pl.ds(k0, tk), :]
        vj = v_ref[pl.ds(k0, tk), :]
        scores = [_dot(kj, qst[:, c * span:(c + 1) * span]) for c in range(n_span)]
        new = []
        for c, (s, (m, l, acc)) in enumerate(zip(scores, carry)):
            if diagonal is not None:
                key = lax.broadcasted_iota(jnp.int32, s.shape, 0) + diagonal * tk
                qry = lax.broadcasted_iota(jnp.int32, s.shape, 1) + (c * span) % tq
                s = jnp.where(key <= qry, s, NEG_INF)
            m_new = jnp.maximum(m, jnp.max(s, axis=0, keepdims=True))
            alpha = jnp.exp(m - m_new)
            p = jnp.exp(s - m_new)
            l_new = l * alpha + jnp.sum(p, axis=0, keepdims=True)
            pv = lax.dot_general(vj, p.astype(BF16), (((0,), (0,)), ((), ())), preferred_element_type=F32)
            new.append((m_new, l_new, acc * alpha + pv))
        return tuple(new)

    init = tuple((jnp.full((1, span), NEG_INF, F32), jnp.zeros((1, span), F32), jnp.zeros((V_DIM, span), F32))
                 for _ in range(n_span))
    fin = lax.fori_loop(0, qi * kt_per_q, update, init)
    for d in range(kt_per_q):
        fin = update(qi * kt_per_q + d, fin, d)
    o = jnp.concatenate([acc / l for _, l, acc in fin], axis=1)
    lam = _lambda_full(lq1_ref, lk1_ref, lq2_ref, lk2_ref, lam_init)
    out = o[:, :tq] - lam * o[:, tq:]
    out = out * lax.rsqrt(jnp.mean(out * out, axis=0, keepdims=True) + EPS) * subln_ref[...] * (1.0 - lam_init)
    o_ref[...] = out.T.astype(BF16)


def _attn_prompt(q, k, v, lams, subln, lam_init, bsz):
    rows = q.shape[0]
    t = rows // bsz
    tq = ATTN_Q_TILE if t % ATTN_Q_TILE == 0 else t
    tk = ATTN_K_TILE if tq % ATTN_K_TILE == 0 else tq
    nq = t // tq
    small = pl.BlockSpec((1, HEAD_DIM), lambda b, h, i: (0, 0))
    return pl.pallas_call(
        functools.partial(_attn_prompt_body, lam_init=lam_init, tk=tk),
        grid=(bsz, N_HEADS, nq),
        in_specs=[pl.BlockSpec((tq, V_DIM), lambda b, h, i: (b * nq + i, h)),
                  pl.BlockSpec((t, V_DIM), lambda b, h, i: (b, h)),
                  pl.BlockSpec((t, V_DIM), lambda b, h, i: (b, h)),
                  small, small, small, small,
                  pl.BlockSpec((V_DIM, 1), lambda b, h, i: (0, 0))],
        out_specs=pl.BlockSpec((tq, V_DIM), lambda b, h, i: (b * nq + i, h)),
        out_shape=jax.ShapeDtypeStruct((rows, D_MODEL), BF16),
        compiler_params=_cparams("parallel", "parallel", "arbitrary"),
        name="attn_prompt",
    )(q, k, v, *[a.astype(F32).reshape(1, HEAD_DIM) for a in lams], subln.astype(F32).reshape(V_DIM, 1))


def _attn_pages_body(pt_ref, qr_ref, *refs):
    del pt_ref
    npg = PAGES_PER_STEP
    k_refs, v_refs = refs[:npg], refs[npg:2 * npg]
    m_ref, l_ref, acc_ref = refs[2 * npg:]
    g = pl.program_id(1)

    @pl.when(g == 0)
    def _():
        m_ref[...] = jnp.full(m_ref.shape, NEG_INF, F32)
        l_ref[...] = jnp.zeros(l_ref.shape, F32)
        acc_ref[...] = jnp.zeros(acc_ref.shape, F32)

    qr = qr_ref[0]
    page = k_refs[0].shape[-1]

    def keys_t(i):
        return k_refs[i][0, 0].reshape(D_MODEL, page).astype(BF16)

    def values(i):
        heads = [v_refs[i][0, 0, pl.ds(h, page, stride=N_HEADS), :] for h in range(N_HEADS)]
        return jnp.concatenate(heads, axis=1).astype(BF16)

    s = jnp.concatenate([_dot(qr, keys_t(i)) for i in range(npg)], axis=1)
    m_old = m_ref[0][:, 0:1]
    l_old = l_ref[0][:, 0:1]
    m_new = jnp.maximum(m_old, jnp.max(s, axis=-1, keepdims=True))
    alpha = jnp.exp(m_old - m_new)
    p = jnp.exp(s - m_new)
    l_new = l_old * alpha + jnp.sum(p, axis=-1, keepdims=True)
    pb = p.astype(BF16)
    pv = _dot(pb[:, 0:page], values(0))
    for i in range(1, npg):
        pv = pv + _dot(pb[:, i * page:(i + 1) * page], values(i))
    acc_ref[0] = acc_ref[0] * alpha + pv
    m_ref[0] = jnp.broadcast_to(m_new, m_ref.shape[1:])
    l_ref[0] = jnp.broadcast_to(l_new, l_ref.shape[1:])


def _attn_pages(q_rows, cache_k, cache_v, layer, page_table):
    b, n_pages = page_table.shape
    page = cache_k.shape[2]
    ck = jnp.transpose(cache_k, (0, 1, 3, 4, 2))
    npg = PAGES_PER_STEP
    assert n_pages % npg == 0

    def k_spec(i):
        return pl.BlockSpec((1, 1, N_SUB, HEAD_DIM, page), lambda s, g, pt: (layer, pt[s, g * npg + i], 0, 0, 0))

    cv = cache_v.reshape(cache_v.shape[:2] + (page * N_HEADS, V_DIM))

    def v_spec(i):
        return pl.BlockSpec((1, 1, page * N_HEADS, V_DIM), lambda s, g, pt: (layer, pt[s, g * npg + i], 0, 0))

    stat_spec = pl.BlockSpec((1, N_SUB, LANES), lambda s, g, pt: (s, 0, 0))
    acc_spec = pl.BlockSpec((1, N_SUB, D_MODEL), lambda s, g, pt: (s, 0, 0))
    grid_spec = pltpu.PrefetchScalarGridSpec(
        num_scalar_prefetch=1,
        grid=(b, n_pages // npg),
        in_specs=[acc_spec] + [k_spec(i) for i in range(npg)] + [v_spec(i) for i in range(npg)],
        out_specs=[stat_spec, stat_spec, acc_spec],
    )
    return pl.pallas_call(
        _attn_pages_body,
        grid_spec=grid_spec,
        out_shape=[jax.ShapeDtypeStruct((b, N_SUB, LANES), F32), jax.ShapeDtypeStruct((b, N_SUB, LANES), F32),
                   jax.ShapeDtypeStruct((b, N_SUB, D_MODEL), F32)],
        compiler_params=_cparams("parallel", "arbitrary"),
        name="attn_pages",
    )(page_table, q_rows, *([ck] * npg), *([cv] * npg))


def _attn_sample_finish_body(qr_ref, kn_ref, vn_ref, m_ref, l_ref, acc_ref,
                             lq1_ref, lk1_ref, lq2_ref, lk2_ref, subln_ref, o_ref, *, lam_init):
    qr = qr_ref[0].astype(F32)
    kn = kn_ref[0].astype(F32)
    vn = vn_ref[0].astype(F32)
    s = jnp.sum(qr * kn, axis=-1, keepdims=True)
    m_old = m_ref[0][:, 0:1]
    l_old = l_ref[0][:, 0:1]
    m_new = jnp.maximum(m_old, s)
    alpha = jnp.exp(m_old - m_new)
    p = jnp.exp(s - m_new)
    l_new = l_old * alpha + p
    acc = acc_ref[0] * alpha + p.astype(BF16).astype(F32) * vn
    o = acc / l_new
    lam = _lambda_full(lq1_ref, lk1_ref, lq2_ref, lk2_ref, lam_init)
    heads = []
    for h in range(N_HEADS):
        lanes = slice(h * V_DIM, (h + 1) * V_DIM)
        heads.append(_merge(o[2 * h:2 * h + 1, lanes], o[2 * h + 1:2 * h + 2, lanes], lam, subln_ref[...], lam_init))
    o_ref[0] = jnp.concatenate(heads, axis=1).astype(BF16)


def _attn_sample_finish(q_rows, k_new, v_new, m, l, acc, lams, subln, lam_init):
    b = q_rows.shape[0]
    seq3 = lambda w: pl.BlockSpec((1, 1, w), lambda s: (s, 0, 0))
    sub3 = lambda w: pl.BlockSpec((1, N_SUB, w), lambda s: (s, 0, 0))
    small = pl.BlockSpec((1, HEAD_DIM), lambda s: (0, 0))
    out = pl.pallas_call(
        functools.partial(_attn_sample_finish_body, lam_init=lam_init),
        grid=(b,),
        in_specs=[sub3(D_MODEL), seq3(D_MODEL), seq3(D_MODEL), sub3(LANES), sub3(LANES), sub3(D_MODEL),
                  small, small, small, small, pl.BlockSpec((1, V_DIM), lambda s: (0, 0))],
        out_specs=seq3(D_MODEL),
        out_shape=jax.ShapeDtypeStruct((b, 1, D_MODEL), BF16),
        compiler_params=_cparams("parallel"),
        name="attn_sample_finish",
    )(q_rows, k_new.reshape(b, 1, D_MODEL), v_new.reshape(b, 1, D_MODEL), m, l, acc,
      *[a.astype(F32).reshape(1, HEAD_DIM) for a in lams], subln.astype(F32).reshape(1, V_DIM))
    return out.reshape(b, D_MODEL)


def _out_proj_body(x_ref, a_ref, w_ref, o_ref):
    o_ref[...] = x_ref[...] + _dot(a_ref[...], w_ref[...])


def _out_proj(x, a, w_o):
    rows = x.shape[0]
    tm = _row_tile(rows)
    row_spec = pl.BlockSpec((tm, D_MODEL), lambda i: (i, 0))
    return pl.pallas_call(
        _out_proj_body,
        grid=(rows // tm,),
        in_specs=[row_spec, row_spec, _resident((D_MODEL, D_MODEL))],
        out_specs=row_spec,
        out_shape=jax.ShapeDtypeStruct((rows, D_MODEL), F32),
        compiler_params=_cparams("parallel"),
        name="out_proj",
    )(x, a, w_o)


def _block_diag_rows(q):
    head = jnp.arange(D_MODEL) // HEAD_DIM
    keep = head[None, :] == jnp.arange(N_SUB)[:, None]
    return jnp.where(keep[None], q[:, None, :], jnp.zeros((), q.dtype))


def kernel(x_prompt, x_sample, state_s5_re, state_s5_im, cache_k, cache_v, page_table, ffn_norm, ffn_w_in, ffn_w_out, mix_norm, s5_lambda_re, s5_lambda_im, s5_log_dt, s5_b_re, s5_b_im, s5_c_re, s5_c_im, s5_d, s5_w_glu, s5_b_glu, attn_w_qkv, attn_q_norm, attn_k_norm, attn_lambda_q1, attn_lambda_k1, attn_lambda_q2, attn_lambda_k2, attn_subln, attn_w_o):
    bsz, seq, _ = x_prompt.shape
    dbs, dseq, _ = x_sample.shape
    assert dseq == 1, "the sample group decodes one token per sequence"
    assert SUBLANES % bsz == 0 and seq % S5_CHUNK == 0
    depth = ffn_norm.shape[0]
    past_len = page_table.shape[1] * cache_k.shape[2]

    xp = x_prompt.reshape(bsz * seq, D_MODEL)
    xs = x_sample.reshape(dbs * dseq, D_MODEL)
    w_in = ffn_w_in.astype(BF16)
    w_out = ffn_w_out.astype(BF16)
    f32n = ffn_norm.astype(F32)
    mixn = mix_norm.astype(F32)

    s5_re_p, s5_im_p, s5_re_s, s5_im_s = [], [], [], []
    k_p, v_p, k_s, v_s = [], [], [], []
    for i in range(depth):
        a = i // N_MIXERS
        if i % N_MIXERS == 0:
            xp, hp = _ffn(xp, f32n[i, 0], w_in[i, 0], w_out[i, 0], mixn[i], F32)
            xs, hs = _ffn(xs, f32n[i, 0], w_in[i, 0], w_out[i, 0], mixn[i], F32)
            prm = (s5_lambda_re[a], s5_lambda_im[a], s5_log_dt[a], s5_b_re[a], s5_b_im[a], s5_c_re[a], s5_c_im[a])
            w_glu = s5_w_glu[a].astype(BF16)
            yp, hr_p, hi_p = _s5_prompt(hp, _s5_chunk_tables(*prm), bsz)
            ys, hr_s, hi_s = _s5_sample(hs, state_s5_re[a], state_s5_im[a], *prm)
            xp = _s5_post(xp, yp, mixn[i], s5_d[a], w_glu, s5_b_glu[a])
            xs = _s5_post(xs, ys, mixn[i], s5_d[a], w_glu, s5_b_glu[a])
            s5_re_p.append(hr_p)
            s5_im_p.append(hi_p)
            s5_re_s.append(hr_s)
            s5_im_s.append(hi_s)
        else:
            xp, hp = _ffn(xp, f32n[i, 0], w_in[i, 0], w_out[i, 0], mixn[i], BF16)
            xs, hs = _ffn(xs, f32n[i, 0], w_in[i, 0], w_out[i, 0], mixn[i], BF16)
            lam_init = 0.8 - 0.6 * math.exp(-0.3 * i)
            lams = (attn_lambda_q1[a], attn_lambda_k1[a], attn_lambda_q2[a], attn_lambda_k2[a])
            w_qkv = attn_w_qkv[a].astype(BF16)
            w_o = attn_w_o[a].astype(BF16)
            cos_p, sin_p = _rope_tables(jnp.arange(seq))
            cos_s, sin_s = _rope_tables(jnp.tile(past_len + jnp.arange(dseq), dbs))
            qp, kpf, kpb, vpf, vpb = _qkv(hp, w_qkv, attn_q_norm[a], attn_k_norm[a], cos_p, sin_p, seq_minor_k=True)
            qs, ksf, ksb, vsf, vsb = _qkv(hs, w_qkv, attn_q_norm[a], attn_k_norm[a], cos_s, sin_s)
            op = _attn_prompt(qp, kpb, vpb, lams, attn_subln[a], lam_init, bsz)
            q_rows = _block_diag_rows(qs)
            m, l, acc = _attn_pages(q_rows, cache_k, cache_v, a, page_table)
            os_ = _attn_sample_finish(q_rows, ksb, vsb, m, l, acc, lams, attn_subln[a], lam_init)
            xp = _out_proj(xp, op, w_o)
            xs = _out_proj(xs, os_, w_o)
            k_p.append(kpf.transpose(0, 3, 1, 2))
            v_p.append(vpf.reshape(bsz, seq, N_HEADS, V_DIM))
            k_s.append(ksf.reshape(dbs, dseq, N_SUB, HEAD_DIM))
            v_s.append(vsf.reshape(dbs, dseq, N_HEADS, V_DIM))
        xp = _ffn(xp, f32n[i, 1], w_in[i, 1], w_out[i, 1])
        xs = _ffn(xs, f32n[i, 1], w_in[i, 1], w_out[i, 1])
    return (xp.reshape(bsz, seq, D_MODEL), xs.reshape(dbs, dseq, D_MODEL),
            jnp.stack(s5_re_p), jnp.stack(s5_im_p), jnp.stack(s5_re_s), jnp.stack(s5_im_s),
            jnp.stack(k_p), jnp.stack(v_p), jnp.stack(k_s), jnp.stack(v_s))
```

```python
import functools
import math

import jax
import jax.numpy as jnp
from jax import lax
from jax.experimental import pallas as pl
from jax.experimental.pallas import tpu as pltpu

F32 = jnp.float32
BF16 = jnp.bfloat16

D_MODEL = 1024
D_FF = 2816
S5_GROUP = 16
S5_GROUPS = D_MODEL // S5_GROUP
S5_STATE = 64
S5_CHUNK = 16
N_HEADS = 8
N_SUB = 2 * N_HEADS
HEAD_DIM = D_MODEL // N_SUB
V_DIM = 2 * HEAD_DIM
ATTN_SCALE = HEAD_DIM ** -0.5
ROPE_THETA = 10000.0
EPS = 1e-6
NEG_INF = -1e30
N_MIXERS = 2

LANES = 128
SUBLANES = 8
MXU_TILE = 256
FF_CHUNK = MXU_TILE
ROW_TILE = 512
ATTN_Q_TILE = 1024
ATTN_K_TILE = 1024
PAGES_PER_STEP = 8
LANE_GROUPS = LANES // S5_GROUP
S5_SEQS_PER_STEP = 2
S5_RELAYOUT_ROWS = 64
VMEM_LIMIT = 48 * 1024 * 1024


def _cparams(*sem):
    return pltpu.CompilerParams(dimension_semantics=sem, vmem_limit_bytes=VMEM_LIMIT)


def _resident(shape):
    zeros = (0,) * len(shape)
    return pl.BlockSpec(shape, lambda *_: zeros, pipeline_mode=pl.Buffered(1))


def _rms(x, g):
    return x * lax.rsqrt(jnp.mean(x * x, axis=-1, keepdims=True) + EPS) * g


def _dot(a, b):
    return jnp.dot(a, b, preferred_element_type=F32)


def _dot_nt(a, b):
    return lax.dot_general(a, b, (((1,), (1,)), ((), ())), preferred_element_type=F32)


def _row_tile(rows):
    return ROW_TILE if rows % ROW_TILE == 0 else rows


def _ffn_body(x_ref, g_ref, win_ref, wout_ref, *rest, norm_dtype):
    if norm_dtype is None:
        o_ref, act_ref = rest
    else:
        gm_ref, o_ref, hn_ref, act_ref = rest
    x = x_ref[...]
    xn = _rms(x, g_ref[...]).astype(BF16)
    for c in range(D_FF // FF_CHUNK):
        lo = c * FF_CHUNK
        g = _dot(xn, win_ref[:, lo:lo + FF_CHUNK])
        u = _dot(xn, win_ref[:, D_FF + lo:D_FF + lo + FF_CHUNK])
        act_ref[:, lo:lo + FF_CHUNK] = (g * jax.nn.sigmoid(g) * u).astype(BF16)
    xo = x + 0.5 * _dot(act_ref[...], wout_ref[...])
    o_ref[...] = xo
    if norm_dtype is not None:
        hn_ref[...] = _rms(xo, gm_ref[...]).astype(norm_dtype)


def _ffn(x, g, w_in, w_out, g_mix=None, norm_dtype=None):
    rows = x.shape[0]
    tm = _row_tile(rows)
    row_spec = pl.BlockSpec((tm, D_MODEL), lambda i: (i, 0))
    in_specs = [row_spec, _resident((1, D_MODEL)), _resident((D_MODEL, 2 * D_FF)), _resident((D_FF, D_MODEL))]
    args = [x, g.reshape(1, D_MODEL), w_in, w_out]
    out_shape = [jax.ShapeDtypeStruct((rows, D_MODEL), F32)]
    out_specs = [row_spec]
    if norm_dtype is not None:
        in_specs.append(_resident((1, D_MODEL)))
        args.append(g_mix.reshape(1, D_MODEL))
        out_shape.append(jax.ShapeDtypeStruct((rows, D_MODEL), norm_dtype))
        out_specs.append(row_spec)
    res = pl.pallas_call(
        functools.partial(_ffn_body, norm_dtype=norm_dtype),
        grid=(rows // tm,),
        in_specs=in_specs,
        out_specs=out_specs,
        out_shape=out_shape,
        scratch_shapes=[pltpu.VMEM((tm, D_FF), BF16)],
        compiler_params=_cparams("parallel"),
        name="ffn_norm" if norm_dtype is not None else "ffn",
    )(*args)
    return res if norm_dtype is not None else res[0]


def _s5_discretise(lam_re, lam_im, log_dt, b_re, b_im):
    dt = jnp.exp(log_dt.astype(F32))[:, None]
    lr = lam_re.astype(F32)
    li = lam_im.astype(F32)
    mag = jnp.exp(lr * dt)
    a_re = mag * jnp.cos(li * dt)
    a_im = mag * jnp.sin(li * dt)
    den = lr * lr + li * li
    nr = a_re - 1.0
    z_re = (nr * lr + a_im * li) / den
    z_im = (a_im * lr - nr * li) / den
    br = b_re.astype(F32)
    bi = b_im.astype(F32)
    bb_re = z_re[..., None] * br - z_im[..., None] * bi
    bb_im = z_re[..., None] * bi + z_im[..., None] * br
    return a_re, a_im, bb_re, bb_im, lr * dt, li * dt


def _s5_chunk_tables(lam_re, lam_im, log_dt, b_re, b_im, c_re, c_im):
    L, G, N, C = S5_CHUNK, S5_GROUPS, S5_STATE, S5_GROUP
    _, _, bb_re, bb_im, ldr, ldi = _s5_discretise(lam_re, lam_im, log_dt, b_re, b_im)
    tau = jnp.arange(L + 1, dtype=F32)[:, None, None]
    mag = jnp.exp(ldr[None] * tau)
    p_re = mag * jnp.cos(ldi[None] * tau)
    p_im = mag * jnp.sin(ldi[None] * tau)
    cr = c_re.astype(F32)
    ci = c_im.astype(F32)
    hi = lax.Precision.HIGHEST
    ca_re = cr[None] * p_re[:, :, None, :] - ci[None] * p_im[:, :, None, :]
    ca_im = cr[None] * p_im[:, :, None, :] + ci[None] * p_re[:, :, None, :]
    ca_rows = jnp.concatenate([ca_re[:L], ca_im[:L]], axis=-1).transpose(1, 3, 0, 2).reshape(G, 2 * N, L * C)
    bb_rows = jnp.concatenate([bb_re, -bb_im], axis=1)
    resp = jnp.einsum('gki,gkx->gix', bb_rows, ca_rows, precision=hi)
    toep = jnp.stack([jnp.pad(resp[:, :, :(L - s) * C], ((0, 0), (0, 0), (s * C, 0))) for s in range(L)], axis=1)
    toep = toep.reshape(G, L * C, L * C)
    pw_re = p_re[L - 1 - jnp.arange(L)]
    pw_im = p_im[L - 1 - jnp.arange(L)]
    w_re = pw_re[..., None] * bb_re[None] - pw_im[..., None] * bb_im[None]
    w_im = pw_re[..., None] * bb_im[None] + pw_im[..., None] * bb_re[None]
    w_re = w_re.transpose(1, 0, 3, 2).reshape(G, L * C, N)
    w_im = w_im.transpose(1, 0, 3, 2).reshape(G, L * C, N)
    w_in = jnp.concatenate([w_re, w_im, w_im, w_re], axis=-1)
    v_re = ca_re[1:].transpose(1, 3, 0, 2).reshape(G, N, L * C)
    v_im = -ca_im[1:].transpose(1, 3, 0, 2).reshape(G, N, L * C)
    v_out = jnp.concatenate([v_re, v_im], axis=1)
    al_re, al_im = p_re[L], p_im[L]
    decay = jnp.stack([jnp.concatenate([al_re, al_re], -1),
                       jnp.concatenate([-al_im, al_im], -1),
                       jnp.concatenate([al_im, -al_im], -1)], axis=1)
    return toep.astype(BF16), w_in.astype(BF16), v_out.astype(BF16), decay


def _block_transpose(xs):
    n = len(xs)
    block = lax.broadcasted_iota(jnp.int32, xs[0].shape, 1) // S5_GROUP
    d = n // 2
    while d >= 1:
        upper = (block // d) % 2 == 1
        nxt = list(xs)
        for i in range(n):
            if i & d == 0:
                a, b = xs[i], xs[i | d]
                nxt[i] = jnp.where(upper, pltpu.roll(b, d * S5_GROUP, 1), a)
                nxt[i | d] = jnp.where(upper, b, pltpu.roll(a, LANES - d * S5_GROUP, 1))
        xs = nxt
        d //= 2
    return xs


def _s5_prompt_body(hn_ref, toep_ref, win_ref, vout_ref, dec_ref, y_ref, hfin_ref, u_scr, sy_scr, hp_scr, *, sps):
    lg = LANE_GROUPS
    rows = u_scr.shape[2]
    nch = rows // sps
    n2 = 2 * S5_STATE
    half = S5_CHUNK // 2
    rb = S5_RELAYOUT_ROWS
    blocks_per_seq = nch // rb

    def chunk_rows(blk):
        seq = blk // blocks_per_seq
        c0 = (blk % blocks_per_seq) * rb
        return c0 * sps + seq, (seq * nch + c0) * S5_CHUNK

    def gather(blk, _):
        r0, t0 = chunk_rows(blk)
        for h in range(2):
            xs = [hn_ref[pl.ds(t0 + h * half + tt, rb, stride=S5_CHUNK), :] for tt in range(half)]
            for g, y in enumerate(_block_transpose(xs)):
                u_scr[g, h, pl.ds(r0, rb, stride=sps), :] = y
        return _

    lax.fori_loop(0, sps * blocks_per_seq, gather, 0)

    def chunk_inputs(g):
        return jnp.concatenate([u_scr[g, 0], u_scr[g, 1]], axis=1).astype(BF16)

    for g in range(lg):
        s = _dot(chunk_inputs(g), win_ref[g])
        sy_scr[g, 0] = s[:, :n2]
        sy_scr[g, 1] = s[:, n2:]
    decs = [dec_ref[g] for g in range(lg)]
    cpv = SUBLANES // sps

    def step(i, carry):
        r0 = pl.multiple_of(i * SUBLANES, SUBLANES)
        new = []
        for g in range(lg):
            h, hs = carry[g]
            a1 = decs[g][0:1, :]
            a2 = decs[g][1:2, :]
            a2s = decs[g][2:3, :]
            slab = sy_scr[g, 0, pl.ds(r0, SUBLANES), :]
            slab_s = sy_scr[g, 1, pl.ds(r0, SUBLANES), :]
            prev = []
            for c in range(cpv):
                prev.append(h)
                rs = slice(c * sps, (c + 1) * sps)
                h, hs = (a1 * h + a2 * hs + slab[rs], a1 * hs + a2s * h + slab_s[rs])
            hp_scr[g, pl.ds(r0, SUBLANES), :] = jnp.concatenate(prev, axis=0)
            new.append((h, hs))
        return tuple(new)

    zero = jnp.zeros((sps, n2), F32)
    fin = lax.fori_loop(0, rows // SUBLANES, step, tuple((zero, zero) for _ in range(lg)))
    for g in range(lg):
        hfin_ref[0, g] = fin[g][0]
        y = _dot(chunk_inputs(g), toep_ref[g]) + _dot(hp_scr[g].astype(BF16), vout_ref[g])
        sy_scr[g, 0] = y[:, :n2]
        sy_scr[g, 1] = y[:, n2:]

    def scatter(blk, _):
        r0, t0 = chunk_rows(blk)
        for h in range(2):
            ys = [sy_scr[g, h, pl.ds(r0, rb, stride=sps), :] for g in range(lg)]
            for tt, x in enumerate(_block_transpose(ys)):
                y_ref[pl.ds(t0 + h * half + tt, rb, stride=S5_CHUNK), :] = x
        return _

    lax.fori_loop(0, sps * blocks_per_seq, scatter, 0)


def _s5_prompt(hn, tables, bsz):
    toep, w_in, v_out, decay = tables
    G, L, N = S5_GROUPS, S5_CHUNK, S5_STATE
    t = hn.shape[0] // bsz
    sps = S5_SEQS_PER_STEP
    assert bsz % sps == 0 and SUBLANES % sps == 0 and (t // L) % S5_RELAYOUT_ROWS == 0
    rows = (t // L) * sps
    lg = LANE_GROUPS
    lc = L * S5_GROUP
    tok_spec = pl.BlockSpec((sps * t, LANES), lambda j, s: (s, j))
    gspec = lambda *shape: pl.BlockSpec((lg,) + shape, lambda j, s: (j,) + (0,) * len(shape))
    y, hfin = pl.pallas_call(
        functools.partial(_s5_prompt_body, sps=sps),
        grid=(G // lg, bsz // sps),
        in_specs=[tok_spec, gspec(lc, lc), gspec(lc, lc), gspec(2 * N, lc), gspec(3, 2 * N)],
        out_specs=[tok_spec, pl.BlockSpec((1, lg, sps, 2 * N), lambda j, s: (s, j, 0, 0))],
        out_shape=[jax.ShapeDtypeStruct(hn.shape, F32), jax.ShapeDtypeStruct((bsz // sps, G, sps, 2 * N), F32)],
        scratch_shapes=[pltpu.VMEM((lg, 2, rows, LANES), F32), pltpu.VMEM((lg, 2, rows, LANES), F32),
                        pltpu.VMEM((lg, rows, 2 * N), F32)],
        compiler_params=_cparams("parallel", "parallel"),
        name="s5_prompt",
    )(hn, toep, w_in, v_out, decay)
    hfin = hfin.transpose(0, 2, 1, 3).reshape(bsz, G, 2 * N)
    return y, hfin[:, :, :N], hfin[:, :, N:]


def _s5_sample_body(u_ref, bbr_ref, bbi_ref, ar_ref, ai_ref, h0r_ref, h0i_ref, cr_ref, ci_ref,
                    y_ref, hr_ref, hi_ref):
    hi = lax.Precision.HIGHEST
    u = u_ref[...]
    bu_re = jnp.einsum('gbc,gnc->gbn', u, bbr_ref[...], precision=hi, preferred_element_type=F32)
    bu_im = jnp.einsum('gbc,gnc->gbn', u, bbi_ref[...], precision=hi, preferred_element_type=F32)
    ar, ai = ar_ref[...], ai_ref[...]
    h0r, h0i = h0r_ref[...], h0i_ref[...]
    hr = bu_re + ar * h0r - ai * h0i
    hi_ = bu_im + ar * h0i + ai * h0r
    hr_ref[...] = hr
    hi_ref[...] = hi_
    y_ref[...] = (jnp.einsum('gbn,gcn->gbc', hr, cr_ref[...], precision=hi, preferred_element_type=F32)
                  - jnp.einsum('gbn,gcn->gbc', hi_, ci_ref[...], precision=hi, preferred_element_type=F32))


def _s5_sample(hs, h0_re, h0_im, lam_re, lam_im, log_dt, b_re, b_im, c_re, c_im):
    G, C, N = S5_GROUPS, S5_GROUP, S5_STATE
    b = hs.shape[0]
    a_re, a_im, bb_re, bb_im, _, _ = _s5_discretise(lam_re, lam_im, log_dt, b_re, b_im)
    u = hs.reshape(b, G, C).transpose(1, 0, 2)
    args = [u, bb_re, bb_im, a_re[:, None, :], a_im[:, None, :],
            h0_re.astype(F32).transpose(1, 0, 2), h0_im.astype(F32).transpose(1, 0, 2),
            c_re.astype(F32), c_im.astype(F32)]
    full = lambda a: pl.BlockSpec(a.shape, lambda i: (0,) * a.ndim)
    outs = [jax.ShapeDtypeStruct((G, b, C), F32), jax.ShapeDtypeStruct((G, b, N), F32),
            jax.ShapeDtypeStruct((G, b, N), F32)]
    y, hr, hi = pl.pallas_call(
        _s5_sample_body,
        grid=(1,),
        in_specs=[full(a) for a in args],
        out_specs=[full(o) for o in outs],
        out_shape=outs,
        compiler_params=_cparams("arbitrary"),
        name="s5_sample",
    )(*args)
    return y.transpose(1, 0, 2).reshape(b, D_MODEL), hr.transpose(1, 0, 2), hi.transpose(1, 0, 2)


def _s5_post_body(x_ref, y_ref, gm_ref, d_ref, w_ref, b_ref, o_ref):
    x = x_ref[...]
    hp = _rms(x, gm_ref[...])
    z = jax.nn.gelu(y_ref[...] + d_ref[...] * hp)
    zz = _dot(z.astype(BF16), w_ref[...]) + b_ref[...]
    o_ref[...] = x + zz[:, :D_MODEL] * jax.nn.sigmoid(zz[:, D_MODEL:])


def _s5_post(x, y, g_mix, d, w_glu, b_glu):
    rows = x.shape[0]
    tm = _row_tile(rows)
    row_spec = pl.BlockSpec((tm, D_MODEL), lambda i: (i, 0))
    return pl.pallas_call(
        _s5_post_body,
        grid=(rows // tm,),
        in_specs=[row_spec, row_spec, _resident((1, D_MODEL)), _resident((1, D_MODEL)),
                  _resident((D_MODEL, 2 * D_MODEL)), _resident((1, 2 * D_MODEL))],
        out_specs=row_spec,
        out_shape=jax.ShapeDtypeStruct((rows, D_MODEL), F32),
        compiler_params=_cparams("parallel"),
        name="s5_post",
    )(x, y, g_mix.reshape(1, D_MODEL), d.astype(F32).reshape(1, D_MODEL), w_glu, b_glu.astype(F32).reshape(1, -1))


def _rope_tables(pos):
    half = HEAD_DIM // 2
    inv = ROPE_THETA ** (-jnp.arange(half, dtype=F32) / half)
    ang = pos.astype(F32)[:, None] * inv[None, :]
    cos = jnp.cos(ang)
    sin = jnp.sin(ang)
    cos_t = jnp.concatenate([cos, cos, cos, cos], axis=-1)
    sin_t = jnp.concatenate([-sin, sin, -sin, sin], axis=-1)
    return cos_t, sin_t


def _head_sum_matrix():
    head = jnp.arange(D_MODEL) // HEAD_DIM
    return (head[:, None] == head[None, :]).astype(BF16)


def _qkv_body(hn_ref, w_ref, qg_ref, kg_ref, hsum_ref, cos_ref, sin_ref,
              q_ref, kf_ref, kb_ref, vf_ref, vb_ref, *, seq_minor_k):
    hn = hn_ref[...]
    cos = cos_ref[...]
    sin = sin_ref[...]
    lane = lax.broadcasted_iota(jnp.int32, cos.shape, 1)
    low_half = (lane % HEAD_DIM) < (HEAD_DIM // 2)

    def norm_rope(x, g):
        ss = _dot((x * x).astype(BF16), hsum_ref[...])
        y = x * lax.rsqrt(ss * (1.0 / HEAD_DIM) + EPS) * g
        outs = []
        for j in range(D_MODEL // LANES):
            t = y[:, j * LANES:(j + 1) * LANES]
            below = pltpu.roll(t, HEAD_DIM // 2, 1)
            above = pltpu.roll(t, LANES - HEAD_DIM // 2, 1)
            outs.append(t * cos + jnp.where(low_half, above, below) * sin)
        return jnp.concatenate(outs, axis=1)

    q = norm_rope(_dot(hn, w_ref[:, 0:D_MODEL]), qg_ref[...])
    q_ref[...] = (q * ATTN_SCALE).astype(BF16)
    k = norm_rope(_dot(hn, w_ref[:, D_MODEL:2 * D_MODEL]), kg_ref[...])
    if seq_minor_k:
        kf_ref[0] = k.T.reshape(kf_ref.shape[1:])
    else:
        kf_ref[...] = k
    kb_ref[...] = k.astype(BF16)
    v = _dot(hn, w_ref[:, 2 * D_MODEL:3 * D_MODEL])
    vf_ref[...] = v
    vb_ref[...] = v.astype(BF16)


def _qkv(hn, w_qkv, q_norm, k_norm, cos_t, sin_t, seq_minor_k=False):
    rows = hn.shape[0]
    tm = _row_tile(rows)
    tab_blocks = cos_t.shape[0] // tm
    row_spec = pl.BlockSpec((tm, D_MODEL), lambda i: (i, 0))
    tab_spec = pl.BlockSpec((tm, LANES), lambda i: (i % tab_blocks, 0))
    gain = lambda g: jnp.tile(g.astype(F32), N_SUB).reshape(1, D_MODEL)
    outs = [jax.ShapeDtypeStruct((rows, D_MODEL), dt) for dt in (BF16, F32, BF16, F32, BF16)]
    out_specs = [row_spec] * 5
    if seq_minor_k:
        period = cos_t.shape[0]
        outs[1] = jax.ShapeDtypeStruct((rows // period, N_SUB, HEAD_DIM, period), F32)
        out_specs[1] = pl.BlockSpec((1, N_SUB, HEAD_DIM, tm), lambda i: (i // tab_blocks, 0, 0, i % tab_blocks))
    return pl.pallas_call(
        functools.partial(_qkv_body, seq_minor_k=seq_minor_k),
        grid=(rows // tm,),
        in_specs=[row_spec, _resident((D_MODEL, 3 * D_MODEL)), _resident((1, D_MODEL)), _resident((1, D_MODEL)),
                  _resident((D_MODEL, D_MODEL)), tab_spec, tab_spec],
        out_specs=out_specs,
        out_shape=outs,
        compiler_params=_cparams("parallel"),
        name="qkv",
    )(hn, w_qkv, gain(q_norm), gain(k_norm), _head_sum_matrix(), cos_t, sin_t)


def _lambda_full(lq1_ref, lk1_ref, lq2_ref, lk2_ref, lam_init):
    s1 = jnp.sum(lq1_ref[...] * lk1_ref[...], axis=-1, keepdims=True)
    s2 = jnp.sum(lq2_ref[...] * lk2_ref[...], axis=-1, keepdims=True)
    return jnp.exp(s1) - jnp.exp(s2) + lam_init


def _merge(o1, o2, lam, subln, lam_init):
    out = o1 - lam * o2
    return _rms(out, subln) * (1.0 - lam_init)


def _attn_prompt_body(q_ref, k_ref, v_ref, lq1_ref, lk1_ref, lq2_ref, lk2_ref, subln_ref, o_ref, *, lam_init, tk, nq):
    tq = q_ref.shape[0]
    qi = pl.program_id(2)
    kt_per_q = tq // tk
    qt = q_ref[...].astype(F32).T.astype(BF16)
    dim = lax.broadcasted_iota(jnp.int32, qt.shape, 0)
    zero = jnp.zeros_like(qt)
    qst = jnp.concatenate([jnp.where(dim < HEAD_DIM, qt, zero), jnp.where(dim >= HEAD_DIM, qt, zero)], axis=1)

    span = MXU_TILE
    n_span = 2 * tq // span

    def visible_keys(c, diagonal):
        if diagonal is None:
            return tk
        return max(0, min(tk, (c * span) % tq + span - diagonal * tk))

    def scores(j, diagonal):
        out = []
        for c in range(n_span):
            nk = visible_keys(c, diagonal)
            out.append(_dot(k_ref[pl.ds(j * tk, nk), :], qst[:, c * span:(c + 1) * span]) if nk else None)
        return out

    def absorb(j, tile_scores, carry, diagonal):
        new = []
        for c, (s, (m, l, acc)) in enumerate(zip(tile_scores, carry)):
            if s is None:
                new.append((m, l, acc))
                continue
            if diagonal is not None:
                key = lax.broadcasted_iota(jnp.int32, s.shape, 0) + diagonal * tk
                qry = lax.broadcasted_iota(jnp.int32, s.shape, 1) + (c * span) % tq
                s = jnp.where(key <= qry, s, NEG_INF)
            m_new = jnp.maximum(m, jnp.max(s, axis=0, keepdims=True))
            alpha = jnp.exp(m - m_new)
            p = jnp.exp(s - m_new)
            l_new = l * alpha + jnp.sum(p, axis=0, keepdims=True)
            vj = v_ref[pl.ds(j * tk, s.shape[0]), :]
            pv = lax.dot_general(vj, p.astype(BF16), (((0,), (0,)), ((), ())), preferred_element_type=F32)
            new.append((m_new, l_new, acc * alpha + pv))
        return new

    def run(n_tiles):
        carry = [(jnp.full((1, span), NEG_INF, F32), jnp.zeros((1, span), F32), jnp.zeros((V_DIM, span), F32))
                 for _ in range(n_span)]
        def diagonal(j):
            d = j - (n_tiles - kt_per_q)
            return d if d >= 0 else None

        tile_scores = scores(0, diagonal(0))
        for j in range(n_tiles):
            next_scores = scores(j + 1, diagonal(j + 1)) if j + 1 < n_tiles else None
            carry = absorb(j, tile_scores, carry, diagonal(j))
            tile_scores = next_scores
        o = jnp.concatenate([acc / l for _, l, acc in carry], axis=1)
        lam = _lambda_full(lq1_ref, lk1_ref, lq2_ref, lk2_ref, lam_init)
        out = o[:, :tq] - lam * o[:, tq:]
        out = out * lax.rsqrt(jnp.mean(out * out, axis=0, keepdims=True) + EPS) * subln_ref[...] * (1.0 - lam_init)
        o_ref[...] = out.T.astype(BF16)

    for n in range(nq):
        pl.when(qi == n)(functools.partial(run, (n + 1) * kt_per_q))


def _attn_prompt(q, k, v, lams, subln, lam_init, bsz):
    rows = q.shape[0]
    t = rows // bsz
    tq = ATTN_Q_TILE if t % ATTN_Q_TILE == 0 else t
    tk = ATTN_K_TILE if tq % ATTN_K_TILE == 0 else tq
    nq = t // tq
    small = pl.BlockSpec((1, HEAD_DIM), lambda b, h, i: (0, 0))
    return pl.pallas_call(
        functools.partial(_attn_prompt_body, lam_init=lam_init, tk=tk, nq=nq),
        grid=(bsz, N_HEADS, nq),
        in_specs=[pl.BlockSpec((tq, V_DIM), lambda b, h, i: (b * nq + i, h)),
                  pl.BlockSpec((t, V_DIM), lambda b, h, i: (b, h)),
                  pl.BlockSpec((t, V_DIM), lambda b, h, i: (b, h)),
                  small, small, small, small,
                  pl.BlockSpec((V_DIM, 1), lambda b, h, i: (0, 0))],
        out_specs=pl.BlockSpec((tq, V_DIM), lambda b, h, i: (b * nq + i, h)),
        out_shape=jax.ShapeDtypeStruct((rows, D_MODEL), BF16),
        compiler_params=_cparams("parallel", "parallel", "arbitrary"),
        name="attn_prompt",
    )(q, k, v, *[a.astype(F32).reshape(1, HEAD_DIM) for a in lams], subln.astype(F32).reshape(V_DIM, 1))


def _attn_pages_body(pt_ref, qr_ref, *refs):
    del pt_ref
    npg = PAGES_PER_STEP
    k_refs, v_refs = refs[:npg], refs[npg:2 * npg]
    m_ref, l_ref, acc_ref = refs[2 * npg:]
    g = pl.program_id(1)

    @pl.when(g == 0)
    def _():
        m_ref[...] = jnp.full(m_ref.shape, NEG_INF, F32)
        l_ref[...] = jnp.zeros(l_ref.shape, F32)
        acc_ref[...] = jnp.zeros(acc_ref.shape, F32)

    qr = qr_ref[0]
    page = k_refs[0].shape[-1]

    def keys_t(i):
        return k_refs[i][0, 0].reshape(D_MODEL, page).astype(BF16)

    def values(i):
        heads = [v_refs[i][0, 0, pl.ds(h, page, stride=N_HEADS), :] for h in range(N_HEADS)]
        return jnp.concatenate(heads, axis=1).astype(BF16)

    s = jnp.concatenate([_dot(qr, keys_t(i)) for i in range(npg)], axis=1)
    m_old = m_ref[0][:, 0:1]
    l_old = l_ref[0][:, 0:1]
    m_new = jnp.maximum(m_old, jnp.max(s, axis=-1, keepdims=True))
    alpha = jnp.exp(m_old - m_new)
    p = jnp.exp(s - m_new)
    l_new = l_old * alpha + jnp.sum(p, axis=-1, keepdims=True)
    pb = p.astype(BF16)
    pv = _dot(pb[:, 0:page], values(0))
    for i in range(1, npg):
        pv = pv + _dot(pb[:, i * page:(i + 1) * page], values(i))
    acc_ref[0] = acc_ref[0] * alpha + pv
    m_ref[0] = jnp.broadcast_to(m_new, m_ref.shape[1:])
    l_ref[0] = jnp.broadcast_to(l_new, l_ref.shape[1:])


def _attn_pages(q_rows, cache_k, cache_v, layer, page_table):
    b, n_pages = page_table.shape
    page = cache_k.shape[2]
    ck = jnp.transpose(cache_k, (0, 1, 3, 4, 2))
    npg = PAGES_PER_STEP
    assert n_pages % npg == 0

    def k_spec(i):
        return pl.BlockSpec((1, 1, N_SUB, HEAD_DIM, page), lambda s, g, pt: (layer, pt[s, g * npg + i], 0, 0, 0))

    cv = cache_v.reshape(cache_v.shape[:2] + (page * N_HEADS, V_DIM))

    def v_spec(i):
        return pl.BlockSpec((1, 1, page * N_HEADS, V_DIM), lambda s, g, pt: (layer, pt[s, g * npg + i], 0, 0))

    stat_spec = pl.BlockSpec((1, N_SUB, LANES), lambda s, g, pt: (s, 0, 0))
    acc_spec = pl.BlockSpec((1, N_SUB, D_MODEL), lambda s, g, pt: (s, 0, 0))
    grid_spec = pltpu.PrefetchScalarGridSpec(
        num_scalar_prefetch=1,
        grid=(b, n_pages // npg),
        in_specs=[acc_spec] + [k_spec(i) for i in range(npg)] + [v_spec(i) for i in range(npg)],
        out_specs=[stat_spec, stat_spec, acc_spec],
    )
    return pl.pallas_call(
        _attn_pages_body,
        grid_spec=grid_spec,
        out_shape=[jax.ShapeDtypeStruct((b, N_SUB, LANES), F32), jax.ShapeDtypeStruct((b, N_SUB, LANES), F32),
                   jax.ShapeDtypeStruct((b, N_SUB, D_MODEL), F32)],
        compiler_params=_cparams("parallel", "arbitrary"),
        name="attn_pages",
    )(page_table, q_rows, *([ck] * npg), *([cv] * npg))


def _attn_sample_finish_body(qr_ref, kn_ref, vn_ref, m_ref, l_ref, acc_ref,
                             lq1_ref, lk1_ref, lq2_ref, lk2_ref, subln_ref, o_ref, *, lam_init):
    qr = qr_ref[0].astype(F32)
    kn = kn_ref[0].astype(F32)
    vn = vn_ref[0].astype(F32)
    s = jnp.sum(qr * kn, axis=-1, keepdims=True)
    m_old = m_ref[0][:, 0:1]
    l_old = l_ref[0][:, 0:1]
    m_new = jnp.maximum(m_old, s)
    alpha = jnp.exp(m_old - m_new)
    p = jnp.exp(s - m_new)
    l_new = l_old * alpha + p
    acc = acc_ref[0] * alpha + p.astype(BF16).astype(F32) * vn
    o = acc / l_new
    lam = _lambda_full(lq1_ref, lk1_ref, lq2_ref, lk2_ref, lam_init)
    heads = []
    for h in range(N_HEADS):
        lanes = slice(h * V_DIM, (h + 1) * V_DIM)
        heads.append(_merge(o[2 * h:2 * h + 1, lanes], o[2 * h + 1:2 * h + 2, lanes], lam, subln_ref[...], lam_init))
    o_ref[0] = jnp.concatenate(heads, axis=1).astype(BF16)


def _attn_sample_finish(q_rows, k_new, v_new, m, l, acc, lams, subln, lam_init):
    b = q_rows.shape[0]
    seq3 = lambda w: pl.BlockSpec((1, 1, w), lambda s: (s, 0, 0))
    sub3 = lambda w: pl.BlockSpec((1, N_SUB, w), lambda s: (s, 0, 0))
    small = pl.BlockSpec((1, HEAD_DIM), lambda s: (0, 0))
    out = pl.pallas_call(
        functools.partial(_attn_sample_finish_body, lam_init=lam_init),
        grid=(b,),
        in_specs=[sub3(D_MODEL), seq3(D_MODEL), seq3(D_MODEL), sub3(LANES), sub3(LANES), sub3(D_MODEL),
                  small, small, small, small, pl.BlockSpec((1, V_DIM), lambda s: (0, 0))],
        out_specs=seq3(D_MODEL),
        out_shape=jax.ShapeDtypeStruct((b, 1, D_MODEL), BF16),
        compiler_params=_cparams("parallel"),
        name="attn_sample_finish",
    )(q_rows, k_new.reshape(b, 1, D_MODEL), v_new.reshape(b, 1, D_MODEL), m, l, acc,
      *[a.astype(F32).reshape(1, HEAD_DIM) for a in lams], subln.astype(F32).reshape(1, V_DIM))
    return out.reshape(b, D_MODEL)


def _out_proj_body(x_ref, a_ref, w_ref, o_ref):
    o_ref[...] = x_ref[...] + _dot(a_ref[...], w_ref[...])


def _out_proj(x, a, w_o):
    rows = x.shape[0]
    tm = _row_tile(rows)
    row_spec = pl.BlockSpec((tm, D_MODEL), lambda i: (i, 0))
    return pl.pallas_call(
        _out_proj_body,
        grid=(rows // tm,),
        in_specs=[row_spec, row_spec, _resident((D_MODEL, D_MODEL))],
        out_specs=row_spec,
        out_shape=jax.ShapeDtypeStruct((rows, D_MODEL), F32),
        compiler_params=_cparams("parallel"),
        name="out_proj",
    )(x, a, w_o)


def _block_diag_rows(q):
    head = jnp.arange(D_MODEL) // HEAD_DIM
    keep = head[None, :] == jnp.arange(N_SUB)[:, None]
    return jnp.where(keep[None], q[:, None, :], jnp.zeros((), q.dtype))


def kernel(x_prompt, x_sample, state_s5_re, state_s5_im, cache_k, cache_v, page_table, ffn_norm, ffn_w_in, ffn_w_out, mix_norm, s5_lambda_re, s5_lambda_im, s5_log_dt, s5_b_re, s5_b_im, s5_c_re, s5_c_im, s5_d, s5_w_glu, s5_b_glu, attn_w_qkv, attn_q_norm, attn_k_norm, attn_lambda_q1, attn_lambda_k1, attn_lambda_q2, attn_lambda_k2, attn_subln, attn_w_o):
    bsz, seq, _ = x_prompt.shape
    dbs, dseq, _ = x_sample.shape
    assert dseq == 1, "the sample group decodes one token per sequence"
    assert SUBLANES % bsz == 0 and seq % S5_CHUNK == 0
    depth = ffn_norm.shape[0]
    past_len = page_table.shape[1] * cache_k.shape[2]

    xp = x_prompt.reshape(bsz * seq, D_MODEL)
    xs = x_sample.reshape(dbs * dseq, D_MODEL)
    w_in = ffn_w_in.astype(BF16)
    w_out = ffn_w_out.astype(BF16)
    f32n = ffn_norm.astype(F32)
    mixn = mix_norm.astype(F32)

    s5_re_p, s5_im_p, s5_re_s, s5_im_s = [], [], [], []
    k_p, v_p, k_s, v_s = [], [], [], []
    for i in range(depth):
        a = i // N_MIXERS
        if i % N_MIXERS == 0:
            xp, hp = _ffn(xp, f32n[i, 0], w_in[i, 0], w_out[i, 0], mixn[i], F32)
            xs, hs = _ffn(xs, f32n[i, 0], w_in[i, 0], w_out[i, 0], mixn[i], F32)
            prm = (s5_lambda_re[a], s5_lambda_im[a], s5_log_dt[a], s5_b_re[a], s5_b_im[a], s5_c_re[a], s5_c_im[a])
            w_glu = s5_w_glu[a].astype(BF16)
            yp, hr_p, hi_p = _s5_prompt(hp, _s5_chunk_tables(*prm), bsz)
            ys, hr_s, hi_s = _s5_sample(hs, state_s5_re[a], state_s5_im[a], *prm)
            xp = _s5_post(xp, yp, mixn[i], s5_d[a], w_glu, s5_b_glu[a])
            xs = _s5_post(xs, ys, mixn[i], s5_d[a], w_glu, s5_b_glu[a])
            s5_re_p.append(hr_p)
            s5_im_p.append(hi_p)
            s5_re_s.append(hr_s)
            s5_im_s.append(hi_s)
        else:
            xp, hp = _ffn(xp, f32n[i, 0], w_in[i, 0], w_out[i, 0], mixn[i], BF16)
            xs, hs = _ffn(xs, f32n[i, 0], w_in[i, 0], w_out[i, 0], mixn[i], BF16)
            lam_init = 0.8 - 0.6 * math.exp(-0.3 * i)
            lams = (attn_lambda_q1[a], attn_lambda_k1[a], attn_lambda_q2[a], attn_lambda_k2[a])
            w_qkv = attn_w_qkv[a].astype(BF16)
            w_o = attn_w_o[a].astype(BF16)
            cos_p, sin_p = _rope_tables(jnp.arange(seq))
            cos_s, sin_s = _rope_tables(jnp.tile(past_len + jnp.arange(dseq), dbs))
            qp, kpf, kpb, vpf, vpb = _qkv(hp, w_qkv, attn_q_norm[a], attn_k_norm[a], cos_p, sin_p, seq_minor_k=True)
            qs, ksf, ksb, vsf, vsb = _qkv(hs, w_qkv, attn_q_norm[a], attn_k_norm[a], cos_s, sin_s)
            op = _attn_prompt(qp, kpb, vpb, lams, attn_subln[a], lam_init, bsz)
            q_rows = _block_diag_rows(qs)
            m, l, acc = _attn_pages(q_rows, cache_k, cache_v, a, page_table)
            os_ = _attn_sample_finish(q_rows, ksb, vsb, m, l, acc, lams, attn_subln[a], lam_init)
            xp = _out_proj(xp, op, w_o)
            xs = _out_proj(xs, os_, w_o)
            k_p.append(kpf.transpose(0, 3, 1, 2))
            v_p.append(vpf.reshape(bsz, seq, N_HEADS, V_DIM))
            k_s.append(ksf.reshape(dbs, dseq, N_SUB, HEAD_DIM))
            v_s.append(vsf.reshape(dbs, dseq, N_HEADS, V_DIM))
        xp = _ffn(xp, f32n[i, 1], w_in[i, 1], w_out[i, 1])
        xs = _ffn(xs, f32n[i, 1], w_in[i, 1], w_out[i, 1])
    return (xp.reshape(bsz, seq, D_MODEL), xs.reshape(dbs, dseq, D_MODEL),
            jnp.stack(s5_re_p), jnp.stack(s5_im_p), jnp.stack(s5_re_s), jnp.stack(s5_im_s),
            jnp.stack(k_p), jnp.stack(v_p), jnp.stack(k_s), jnp.stack(v_s))
```

```python
import functools
import math

import jax
import jax.numpy as jnp
from jax import lax
from jax.experimental import pallas as pl
from jax.experimental.pallas import tpu as pltpu

F32 = jnp.float32
BF16 = jnp.bfloat16

D_MODEL = 1024
D_FF = 2816
S5_GROUP = 16
S5_GROUPS = D_MODEL // S5_GROUP
S5_STATE = 64
S5_CHUNK = 16
N_HEADS = 8
N_SUB = 2 * N_HEADS
HEAD_DIM = D_MODEL // N_SUB
V_DIM = 2 * HEAD_DIM
ATTN_SCALE = HEAD_DIM ** -0.5
ROPE_THETA = 10000.0
EPS = 1e-6
NEG_INF = -1e30
N_MIXERS = 2

LANES = 128
SUBLANES = 8
MXU_TILE = 256
FF_CHUNK = MXU_TILE
ROW_TILE = 512
ATTN_Q_TILE = 1024
ATTN_K_TILE = 1024
HOSTED_GROUP_PAGES = 4
LANE_GROUPS = LANES // S5_GROUP
S5_SEQS_PER_STEP = 2
S5_RELAYOUT_ROWS = 64
VMEM_LIMIT = 48 * 1024 * 1024


def _cparams(*sem):
    return pltpu.CompilerParams(dimension_semantics=sem, vmem_limit_bytes=VMEM_LIMIT)


def _resident(shape):
    zeros = (0,) * len(shape)
    return pl.BlockSpec(shape, lambda *_: zeros, pipeline_mode=pl.Buffered(1))


def _rms(x, g):
    return x * lax.rsqrt(jnp.mean(x * x, axis=-1, keepdims=True) + EPS) * g


def _dot(a, b):
    return jnp.dot(a, b, preferred_element_type=F32)


def _dot_nt(a, b):
    return lax.dot_general(a, b, (((1,), (1,)), ((), ())), preferred_element_type=F32)


def _row_tile(rows):
    return ROW_TILE if rows % ROW_TILE == 0 else rows


def _ffn_body(x_ref, g_ref, win_ref, wout_ref, *rest, norm_dtype):
    if norm_dtype is None:
        o_ref, act_ref = rest
    else:
        gm_ref, o_ref, hn_ref, act_ref = rest
    x = x_ref[...]
    xn = _rms(x, g_ref[...]).astype(BF16)
    for c in range(D_FF // FF_CHUNK):
        lo = c * FF_CHUNK
        g = _dot(xn, win_ref[:, lo:lo + FF_CHUNK])
        u = _dot(xn, win_ref[:, D_FF + lo:D_FF + lo + FF_CHUNK])
        act_ref[:, lo:lo + FF_CHUNK] = (g * jax.nn.sigmoid(g) * u).astype(BF16)
    xo = x + 0.5 * _dot(act_ref[...], wout_ref[...])
    o_ref[...] = xo
    if norm_dtype is not None:
        hn_ref[...] = _rms(xo, gm_ref[...]).astype(norm_dtype)


def _ffn(x, g, w_in, w_out, g_mix=None, norm_dtype=None):
    rows = x.shape[0]
    tm = _row_tile(rows)
    row_spec = pl.BlockSpec((tm, D_MODEL), lambda i: (i, 0))
    in_specs = [row_spec, _resident((1, D_MODEL)), _resident((D_MODEL, 2 * D_FF)), _resident((D_FF, D_MODEL))]
    args = [x, g.reshape(1, D_MODEL), w_in, w_out]
    out_shape = [jax.ShapeDtypeStruct((rows, D_MODEL), F32)]
    out_specs = [row_spec]
    if norm_dtype is not None:
        in_specs.append(_resident((1, D_MODEL)))
        args.append(g_mix.reshape(1, D_MODEL))
        out_shape.append(jax.ShapeDtypeStruct((rows, D_MODEL), norm_dtype))
        out_specs.append(row_spec)
    res = pl.pallas_call(
        functools.partial(_ffn_body, norm_dtype=norm_dtype),
        grid=(rows // tm,),
        in_specs=in_specs,
        out_specs=out_specs,
        out_shape=out_shape,
        scratch_shapes=[pltpu.VMEM((tm, D_FF), BF16)],
        compiler_params=_cparams("parallel"),
        name="ffn_norm" if norm_dtype is not None else "ffn",
    )(*args)
    return res if norm_dtype is not None else res[0]


def _s5_discretise(lam_re, lam_im, log_dt, b_re, b_im):
    dt = jnp.exp(log_dt.astype(F32))[:, None]
    lr = lam_re.astype(F32)
    li = lam_im.astype(F32)
    mag = jnp.exp(lr * dt)
    a_re = mag * jnp.cos(li * dt)
    a_im = mag * jnp.sin(li * dt)
    den = lr * lr + li * li
    nr = a_re - 1.0
    z_re = (nr * lr + a_im * li) / den
    z_im = (a_im * lr - nr * li) / den
    br = b_re.astype(F32)
    bi = b_im.astype(F32)
    bb_re = z_re[..., None] * br - z_im[..., None] * bi
    bb_im = z_re[..., None] * bi + z_im[..., None] * br
    return a_re, a_im, bb_re, bb_im, lr * dt, li * dt


def _s5_chunk_tables(lam_re, lam_im, log_dt, b_re, b_im, c_re, c_im):
    L, G, N, C = S5_CHUNK, S5_GROUPS, S5_STATE, S5_GROUP
    _, _, bb_re, bb_im, ldr, ldi = _s5_discretise(lam_re, lam_im, log_dt, b_re, b_im)
    tau = jnp.arange(L + 1, dtype=F32)[:, None, None]
    mag = jnp.exp(ldr[None] * tau)
    p_re = mag * jnp.cos(ldi[None] * tau)
    p_im = mag * jnp.sin(ldi[None] * tau)
    cr = c_re.astype(F32)
    ci = c_im.astype(F32)
    hi = lax.Precision.HIGHEST
    ca_re = cr[None] * p_re[:, :, None, :] - ci[None] * p_im[:, :, None, :]
    ca_im = cr[None] * p_im[:, :, None, :] + ci[None] * p_re[:, :, None, :]
    ca_rows = jnp.concatenate([ca_re[:L], ca_im[:L]], axis=-1).transpose(1, 3, 0, 2).reshape(G, 2 * N, L * C)
    bb_rows = jnp.concatenate([bb_re, -bb_im], axis=1)
    resp = jnp.einsum('gki,gkx->gix', bb_rows, ca_rows, precision=hi)
    toep = jnp.stack([jnp.pad(resp[:, :, :(L - s) * C], ((0, 0), (0, 0), (s * C, 0))) for s in range(L)], axis=1)
    toep = toep.reshape(G, L * C, L * C)
    pw_re = p_re[L - 1 - jnp.arange(L)]
    pw_im = p_im[L - 1 - jnp.arange(L)]
    w_re = pw_re[..., None] * bb_re[None] - pw_im[..., None] * bb_im[None]
    w_im = pw_re[..., None] * bb_im[None] + pw_im[..., None] * bb_re[None]
    w_re = w_re.transpose(1, 0, 3, 2).reshape(G, L * C, N)
    w_im = w_im.transpose(1, 0, 3, 2).reshape(G, L * C, N)
    w_in = jnp.concatenate([w_re, w_im, w_im, w_re], axis=-1)
    v_re = ca_re[1:].transpose(1, 3, 0, 2).reshape(G, N, L * C)
    v_im = -ca_im[1:].transpose(1, 3, 0, 2).reshape(G, N, L * C)
    v_out = jnp.concatenate([v_re, v_im], axis=1)
    al_re, al_im = p_re[L], p_im[L]
    decay = jnp.stack([jnp.concatenate([al_re, al_re], -1),
                       jnp.concatenate([-al_im, al_im], -1),
                       jnp.concatenate([al_im, -al_im], -1)], axis=1)
    return toep.astype(BF16), w_in.astype(BF16), v_out.astype(BF16), decay


def _block_transpose(xs):
    n = len(xs)
    block = lax.broadcasted_iota(jnp.int32, xs[0].shape, 1) // S5_GROUP
    d = n // 2
    while d >= 1:
        upper = (block // d) % 2 == 1
        nxt = list(xs)
        for i in range(n):
            if i & d == 0:
                a, b = xs[i], xs[i | d]
                nxt[i] = jnp.where(upper, pltpu.roll(b, d * S5_GROUP, 1), a)
                nxt[i | d] = jnp.where(upper, b, pltpu.roll(a, LANES - d * S5_GROUP, 1))
        xs = nxt
        d //= 2
    return xs


def _s5_prompt_body(hn_ref, toep_ref, win_ref, vout_ref, dec_ref, y_ref, hfin_ref, u_scr, sy_scr, hp_scr, *, sps):
    lg = LANE_GROUPS
    rows = u_scr.shape[2]
    nch = rows // sps
    n2 = 2 * S5_STATE
    half = S5_CHUNK // 2
    rb = S5_RELAYOUT_ROWS
    blocks_per_seq = nch // rb

    def chunk_rows(blk):
        seq = blk // blocks_per_seq
        c0 = (blk % blocks_per_seq) * rb
        return c0 * sps + seq, (seq * nch + c0) * S5_CHUNK

    def gather(blk, _):
        r0, t0 = chunk_rows(blk)
        for h in range(2):
            xs = [hn_ref[pl.ds(t0 + h * half + tt, rb, stride=S5_CHUNK), :] for tt in range(half)]
            for g, y in enumerate(_block_transpose(xs)):
                u_scr[g, h, pl.ds(r0, rb, stride=sps), :] = y
        return _

    lax.fori_loop(0, sps * blocks_per_seq, gather, 0)

    def chunk_inputs(g):
        return jnp.concatenate([u_scr[g, 0], u_scr[g, 1]], axis=1).astype(BF16)

    for g in range(lg):
        s = _dot(chunk_inputs(g), win_ref[g])
        sy_scr[g, 0] = s[:, :n2]
        sy_scr[g, 1] = s[:, n2:]
    decs = [dec_ref[g] for g in range(lg)]
    cpv = SUBLANES // sps

    def step(i, carry):
        r0 = pl.multiple_of(i * SUBLANES, SUBLANES)
        new = []
        for g in range(lg):
            h, hs = carry[g]
            a1 = decs[g][0:1, :]
            a2 = decs[g][1:2, :]
            a2s = decs[g][2:3, :]
            slab = sy_scr[g, 0, pl.ds(r0, SUBLANES), :]
            slab_s = sy_scr[g, 1, pl.ds(r0, SUBLANES), :]
            prev = []
            for c in range(cpv):
                prev.append(h)
                rs = slice(c * sps, (c + 1) * sps)
                h, hs = (a1 * h + a2 * hs + slab[rs], a1 * hs + a2s * h + slab_s[rs])
            hp_scr[g, pl.ds(r0, SUBLANES), :] = jnp.concatenate(prev, axis=0)
            new.append((h, hs))
        return tuple(new)

    zero = jnp.zeros((sps, n2), F32)
    fin = lax.fori_loop(0, rows // SUBLANES, step, tuple((zero, zero) for _ in range(lg)))
    for g in range(lg):
        hfin_ref[0, g] = fin[g][0]
        y = _dot(chunk_inputs(g), toep_ref[g]) + _dot(hp_scr[g].astype(BF16), vout_ref[g])
        sy_scr[g, 0] = y[:, :n2]
        sy_scr[g, 1] = y[:, n2:]

    def scatter(blk, _):
        r0, t0 = chunk_rows(blk)
        for h in range(2):
            ys = [sy_scr[g, h, pl.ds(r0, rb, stride=sps), :] for g in range(lg)]
            for tt, x in enumerate(_block_transpose(ys)):
                y_ref[pl.ds(t0 + h * half + tt, rb, stride=S5_CHUNK), :] = x
        return _

    lax.fori_loop(0, sps * blocks_per_seq, scatter, 0)


def _s5_prompt(hn, tables, bsz):
    toep, w_in, v_out, decay = tables
    G, L, N = S5_GROUPS, S5_CHUNK, S5_STATE
    t = hn.shape[0] // bsz
    sps = S5_SEQS_PER_STEP
    assert bsz % sps == 0 and SUBLANES % sps == 0 and (t // L) % S5_RELAYOUT_ROWS == 0
    rows = (t // L) * sps
    lg = LANE_GROUPS
    lc = L * S5_GROUP
    tok_spec = pl.BlockSpec((sps * t, LANES), lambda j, s: (s, j))
    gspec = lambda *shape: pl.BlockSpec((lg,) + shape, lambda j, s: (j,) + (0,) * len(shape))
    y, hfin = pl.pallas_call(
        functools.partial(_s5_prompt_body, sps=sps),
        grid=(G // lg, bsz // sps),
        in_specs=[tok_spec, gspec(lc, lc), gspec(lc, lc), gspec(2 * N, lc), gspec(3, 2 * N)],
        out_specs=[tok_spec, pl.BlockSpec((1, lg, sps, 2 * N), lambda j, s: (s, j, 0, 0))],
        out_shape=[jax.ShapeDtypeStruct(hn.shape, F32), jax.ShapeDtypeStruct((bsz // sps, G, sps, 2 * N), F32)],
        scratch_shapes=[pltpu.VMEM((lg, 2, rows, LANES), F32), pltpu.VMEM((lg, 2, rows, LANES), F32),
                        pltpu.VMEM((lg, rows, 2 * N), F32)],
        compiler_params=_cparams("parallel", "parallel"),
        name="s5_prompt",
    )(hn, toep, w_in, v_out, decay)
    hfin = hfin.transpose(0, 2, 1, 3).reshape(bsz, G, 2 * N)
    return y, hfin[:, :, :N], hfin[:, :, N:]


def _s5_sample_body(u_ref, bbr_ref, bbi_ref, ar_ref, ai_ref, h0r_ref, h0i_ref, cr_ref, ci_ref,
                    y_ref, hr_ref, hi_ref):
    hi = lax.Precision.HIGHEST
    u = u_ref[...]
    bu_re = jnp.einsum('gbc,gnc->gbn', u, bbr_ref[...], precision=hi, preferred_element_type=F32)
    bu_im = jnp.einsum('gbc,gnc->gbn', u, bbi_ref[...], precision=hi, preferred_element_type=F32)
    ar, ai = ar_ref[...], ai_ref[...]
    h0r, h0i = h0r_ref[...], h0i_ref[...]
    hr = bu_re + ar * h0r - ai * h0i
    hi_ = bu_im + ar * h0i + ai * h0r
    hr_ref[...] = hr
    hi_ref[...] = hi_
    y_ref[...] = (jnp.einsum('gbn,gcn->gbc', hr, cr_ref[...], precision=hi, preferred_element_type=F32)
                  - jnp.einsum('gbn,gcn->gbc', hi_, ci_ref[...], precision=hi, preferred_element_type=F32))


def _s5_sample(hs, h0_re, h0_im, lam_re, lam_im, log_dt, b_re, b_im, c_re, c_im):
    G, C, N = S5_GROUPS, S5_GROUP, S5_STATE
    b = hs.shape[0]
    a_re, a_im, bb_re, bb_im, _, _ = _s5_discretise(lam_re, lam_im, log_dt, b_re, b_im)
    u = hs.reshape(b, G, C).transpose(1, 0, 2)
    args = [u, bb_re, bb_im, a_re[:, None, :], a_im[:, None, :],
            h0_re.astype(F32).transpose(1, 0, 2), h0_im.astype(F32).transpose(1, 0, 2),
            c_re.astype(F32), c_im.astype(F32)]
    full = lambda a: pl.BlockSpec(a.shape, lambda i: (0,) * a.ndim)
    outs = [jax.ShapeDtypeStruct((G, b, C), F32), jax.ShapeDtypeStruct((G, b, N), F32),
            jax.ShapeDtypeStruct((G, b, N), F32)]
    y, hr, hi = pl.pallas_call(
        _s5_sample_body,
        grid=(1,),
        in_specs=[full(a) for a in args],
        out_specs=[full(o) for o in outs],
        out_shape=outs,
        compiler_params=_cparams("arbitrary"),
        name="s5_sample",
    )(*args)
    return y.transpose(1, 0, 2).reshape(b, D_MODEL), hr.transpose(1, 0, 2), hi.transpose(1, 0, 2)


def _s5_post_body(x_ref, y_ref, gm_ref, d_ref, w_ref, b_ref, o_ref):
    x = x_ref[...]
    hp = _rms(x, gm_ref[...])
    z = jax.nn.gelu(y_ref[...] + d_ref[...] * hp)
    zz = _dot(z.astype(BF16), w_ref[...]) + b_ref[...]
    o_ref[...] = x + zz[:, :D_MODEL] * jax.nn.sigmoid(zz[:, D_MODEL:])


def _s5_post(x, y, g_mix, d, w_glu, b_glu):
    rows = x.shape[0]
    tm = _row_tile(rows)
    row_spec = pl.BlockSpec((tm, D_MODEL), lambda i: (i, 0))
    return pl.pallas_call(
        _s5_post_body,
        grid=(rows // tm,),
        in_specs=[row_spec, row_spec, _resident((1, D_MODEL)), _resident((1, D_MODEL)),
                  _resident((D_MODEL, 2 * D_MODEL)), _resident((1, 2 * D_MODEL))],
        out_specs=row_spec,
        out_shape=jax.ShapeDtypeStruct((rows, D_MODEL), F32),
        compiler_params=_cparams("parallel"),
        name="s5_post",
    )(x, y, g_mix.reshape(1, D_MODEL), d.astype(F32).reshape(1, D_MODEL), w_glu, b_glu.astype(F32).reshape(1, -1))


def _rope_tables(pos):
    half = HEAD_DIM // 2
    inv = ROPE_THETA ** (-jnp.arange(half, dtype=F32) / half)
    ang = pos.astype(F32)[:, None] * inv[None, :]
    cos = jnp.cos(ang)
    sin = jnp.sin(ang)
    cos_t = jnp.concatenate([cos, cos, cos, cos], axis=-1)
    sin_t = jnp.concatenate([-sin, sin, -sin, sin], axis=-1)
    return cos_t, sin_t


def _head_sum_matrix():
    head = jnp.arange(D_MODEL) // HEAD_DIM
    return (head[:, None] == head[None, :]).astype(BF16)


def _qkv_body(hn_ref, w_ref, qg_ref, kg_ref, hsum_ref, cos_ref, sin_ref,
              q_ref, kf_ref, kb_ref, vf_ref, vb_ref, *, seq_minor_k):
    hn = hn_ref[...]
    cos = cos_ref[...]
    sin = sin_ref[...]
    lane = lax.broadcasted_iota(jnp.int32, cos.shape, 1)
    low_half = (lane % HEAD_DIM) < (HEAD_DIM // 2)

    def norm_rope(x, g):
        ss = _dot((x * x).astype(BF16), hsum_ref[...])
        y = x * lax.rsqrt(ss * (1.0 / HEAD_DIM) + EPS) * g
        outs = []
        for j in range(D_MODEL // LANES):
            t = y[:, j * LANES:(j + 1) * LANES]
            below = pltpu.roll(t, HEAD_DIM // 2, 1)
            above = pltpu.roll(t, LANES - HEAD_DIM // 2, 1)
            outs.append(t * cos + jnp.where(low_half, above, below) * sin)
        return jnp.concatenate(outs, axis=1)

    q = norm_rope(_dot(hn, w_ref[:, 0:D_MODEL]), qg_ref[...])
    q_ref[...] = (q * ATTN_SCALE).astype(BF16)
    k = norm_rope(_dot(hn, w_ref[:, D_MODEL:2 * D_MODEL]), kg_ref[...])
    if seq_minor_k:
        kf_ref[0] = k.T.reshape(kf_ref.shape[1:])
    else:
        kf_ref[...] = k
    kb_ref[...] = k.astype(BF16)
    v = _dot(hn, w_ref[:, 2 * D_MODEL:3 * D_MODEL])
    vf_ref[...] = v
    vb_ref[...] = v.astype(BF16)


def _qkv(hn, w_qkv, q_norm, k_norm, cos_t, sin_t, seq_minor_k=False):
    rows = hn.shape[0]
    tm = _row_tile(rows)
    tab_blocks = cos_t.shape[0] // tm
    row_spec = pl.BlockSpec((tm, D_MODEL), lambda i: (i, 0))
    tab_spec = pl.BlockSpec((tm, LANES), lambda i: (i % tab_blocks, 0))
    gain = lambda g: jnp.tile(g.astype(F32), N_SUB).reshape(1, D_MODEL)
    outs = [jax.ShapeDtypeStruct((rows, D_MODEL), dt) for dt in (BF16, F32, BF16, F32, BF16)]
    out_specs = [row_spec] * 5
    if seq_minor_k:
        period = cos_t.shape[0]
        outs[1] = jax.ShapeDtypeStruct((rows // period, N_SUB, HEAD_DIM, period), F32)
        out_specs[1] = pl.BlockSpec((1, N_SUB, HEAD_DIM, tm), lambda i: (i // tab_blocks, 0, 0, i % tab_blocks))
    return pl.pallas_call(
        functools.partial(_qkv_body, seq_minor_k=seq_minor_k),
        grid=(rows // tm,),
        in_specs=[row_spec, _resident((D_MODEL, 3 * D_MODEL)), _resident((1, D_MODEL)), _resident((1, D_MODEL)),
                  _resident((D_MODEL, D_MODEL)), tab_spec, tab_spec],
        out_specs=out_specs,
        out_shape=outs,
        compiler_params=_cparams("parallel"),
        name="qkv",
    )(hn, w_qkv, gain(q_norm), gain(k_norm), _head_sum_matrix(), cos_t, sin_t)


def _lambda_full(lq1_ref, lk1_ref, lq2_ref, lk2_ref, lam_init):
    s1 = jnp.sum(lq1_ref[...] * lk1_ref[...], axis=-1, keepdims=True)
    s2 = jnp.sum(lq2_ref[...] * lk2_ref[...], axis=-1, keepdims=True)
    return jnp.exp(s1) - jnp.exp(s2) + lam_init


def _merge(o1, o2, lam, subln, lam_init):
    out = o1 - lam * o2
    return _rms(out, subln) * (1.0 - lam_init)


def _attn_prompt_body(q_ref, k_ref, v_ref, lq1_ref, lk1_ref, lq2_ref, lk2_ref, subln_ref, o_ref, *, lam_init, tk, nq):
    tq = q_ref.shape[0]
    qi = pl.program_id(2)
    kt_per_q = tq // tk
    qt = q_ref[...].astype(F32).T.astype(BF16)
    dim = lax.broadcasted_iota(jnp.int32, qt.shape, 0)
    zero = jnp.zeros_like(qt)
    qst = jnp.concatenate([jnp.where(dim < HEAD_DIM, qt, zero), jnp.where(dim >= HEAD_DIM, qt, zero)], axis=1)

    span = MXU_TILE
    n_span = 2 * tq // span

    def visible_keys(c, diagonal):
        if diagonal is None:
            return tk
        return max(0, min(tk, (c * span) % tq + span - diagonal * tk))

    def scores(j, diagonal):
        out = []
        for c in range(n_span):
            nk = visible_keys(c, diagonal)
            out.append(_dot(k_ref[pl.ds(j * tk, nk), :], qst[:, c * span:(c + 1) * span]) if nk else None)
        return out

    def absorb(j, tile_scores, carry, diagonal):
        new = []
        for c, (s, (m, l, acc)) in enumerate(zip(tile_scores, carry)):
            if s is None:
                new.append((m, l, acc))
                continue
            if diagonal is not None:
                key = lax.broadcasted_iota(jnp.int32, s.shape, 0) + diagonal * tk
                qry = lax.broadcasted_iota(jnp.int32, s.shape, 1) + (c * span) % tq
                s = jnp.where(key <= qry, s, NEG_INF)
            m_new = jnp.maximum(m, jnp.max(s, axis=0, keepdims=True))
            alpha = jnp.exp(m - m_new)
            p = jnp.exp(s - m_new)
            l_new = l * alpha + jnp.sum(p, axis=0, keepdims=True)
            vj = v_ref[pl.ds(j * tk, s.shape[0]), :]
            pv = lax.dot_general(vj, p.astype(BF16), (((0,), (0,)), ((), ())), preferred_element_type=F32)
            new.append((m_new, l_new, acc * alpha + pv))
        return new

    def run(n_tiles):
        carry = [(jnp.full((1, span), NEG_INF, F32), jnp.zeros((1, span), F32), jnp.zeros((V_DIM, span), F32))
                 for _ in range(n_span)]
        def diagonal(j):
            d = j - (n_tiles - kt_per_q)
            return d if d >= 0 else None

        tile_scores = scores(0, diagonal(0))
        for j in range(n_tiles):
            next_scores = scores(j + 1, diagonal(j + 1)) if j + 1 < n_tiles else None
            carry = absorb(j, tile_scores, carry, diagonal(j))
            tile_scores = next_scores
        o = jnp.concatenate([acc / l for _, l, acc in carry], axis=1)
        lam = _lambda_full(lq1_ref, lk1_ref, lq2_ref, lk2_ref, lam_init)
        out = o[:, :tq] - lam * o[:, tq:]
        out = out * lax.rsqrt(jnp.mean(out * out, axis=0, keepdims=True) + EPS) * subln_ref[...] * (1.0 - lam_init)
        o_ref[...] = out.T.astype(BF16)

    for n in range(nq):
        pl.when(qi == n)(functools.partial(run, (n + 1) * kt_per_q))


def _attn_prompt(q, k, v, lams, subln, lam_init, bsz):
    rows = q.shape[0]
    t = rows // bsz
    tq = ATTN_Q_TILE if t % ATTN_Q_TILE == 0 else t
    tk = ATTN_K_TILE if tq % ATTN_K_TILE == 0 else tq
    nq = t // tq
    small = pl.BlockSpec((1, HEAD_DIM), lambda b, h, i: (0, 0))
    return pl.pallas_call(
        functools.partial(_attn_prompt_body, lam_init=lam_init, tk=tk, nq=nq),
        grid=(bsz, N_HEADS, nq),
        in_specs=[pl.BlockSpec((tq, V_DIM), lambda b, h, i: (b * nq + i, h)),
                  pl.BlockSpec((t, V_DIM), lambda b, h, i: (b, h)),
                  pl.BlockSpec((t, V_DIM), lambda b, h, i: (b, h)),
                  small, small, small, small,
                  pl.BlockSpec((V_DIM, 1), lambda b, h, i: (0, 0))],
        out_specs=pl.BlockSpec((tq, V_DIM), lambda b, h, i: (b * nq + i, h)),
        out_shape=jax.ShapeDtypeStruct((rows, D_MODEL), BF16),
        compiler_params=_cparams("parallel", "parallel", "arbitrary"),
        name="attn_prompt",
    )(q, k, v, *[a.astype(F32).reshape(1, HEAD_DIM) for a in lams], subln.astype(F32).reshape(V_DIM, 1))


def _pages_update(qr, keys_t, values, n, page, m_ref, l_ref, acc_ref):
    s = jnp.concatenate([_dot(qr, keys_t(i)) for i in range(n)], axis=1)
    m_old = m_ref[0][:, 0:1]
    l_old = l_ref[0][:, 0:1]
    m_new = jnp.maximum(m_old, jnp.max(s, axis=-1, keepdims=True))
    alpha = jnp.exp(m_old - m_new)
    p = jnp.exp(s - m_new)
    l_new = l_old * alpha + jnp.sum(p, axis=-1, keepdims=True)
    pb = p.astype(BF16)
    pv = _dot(pb[:, 0:page], values(0))
    for i in range(1, n):
        pv = pv + _dot(pb[:, i * page:(i + 1) * page], values(i))
    acc_ref[0] = acc_ref[0] * alpha + pv
    m_ref[0] = jnp.broadcast_to(m_new, m_ref.shape[1:])
    l_ref[0] = jnp.broadcast_to(l_new, l_ref.shape[1:])


def _ffn_pages_body(pt_ref, x_ref, g_ref, win_ref, wout_ref, *rest, norm_dtype, layer, seq0, steps_per_seq, groups):
    if norm_dtype is None:
        qr_ref, ck_ref, cv_ref, o_ref, m_ref, l_ref, acc_ref, act_ref, kbuf, vbuf, sem = rest
    else:
        gm_ref, qr_ref, ck_ref, cv_ref, o_ref, hn_ref, m_ref, l_ref, acc_ref, act_ref, kbuf, vbuf, sem = rest
    i = pl.program_id(0)
    last = pl.num_programs(0) - 1
    gp = HOSTED_GROUP_PAGES
    page = kbuf.shape[-1]

    def copies(step, grp, slot):
        seq = seq0 + step // steps_per_seq
        first = (step % steps_per_seq) * (groups * gp) + grp * gp
        out = []
        for pg in range(gp):
            phys = pt_ref[seq, first + pg]
            out.append(pltpu.make_async_copy(ck_ref.at[layer, phys], kbuf.at[slot, pg], sem.at[slot, 0]))
            out.append(pltpu.make_async_copy(cv_ref.at[layer, phys], vbuf.at[slot, pg], sem.at[slot, 1]))
        return out

    def start(step, grp, slot):
        for c in copies(step, grp, slot):
            c.start()

    @pl.when(i == 0)
    def _():
        start(i, 0, 0)

    @pl.when(i % steps_per_seq == 0)
    def _():
        m_ref[...] = jnp.full(m_ref.shape, NEG_INF, F32)
        l_ref[...] = jnp.zeros(l_ref.shape, F32)
        acc_ref[...] = jnp.zeros(acc_ref.shape, F32)

    x = x_ref[...]
    xn = _rms(x, g_ref[...]).astype(BF16)
    qr = qr_ref[0]
    n_chunks = D_FF // FF_CHUNK
    for grp in range(groups):
        slot = grp % 2
        if grp + 1 < groups:
            start(i, grp + 1, 1 - slot)
        else:
            @pl.when(i < last)
            def _():
                start(i + 1, 0, 1 - slot)
        for c in copies(i, grp, slot):
            c.wait()

        def keys_t(pg, slot=slot):
            return kbuf[slot, pg].reshape(D_MODEL, page).astype(BF16)

        def values(pg, slot=slot):
            heads = [vbuf[slot, pg, pl.ds(h, page, stride=N_HEADS), :] for h in range(N_HEADS)]
            return jnp.concatenate(heads, axis=1).astype(BF16)

        for c in range(grp * n_chunks // groups, (grp + 1) * n_chunks // groups):
            lo = c * FF_CHUNK
            g = _dot(xn, win_ref[:, lo:lo + FF_CHUNK])
            u = _dot(xn, win_ref[:, D_FF + lo:D_FF + lo + FF_CHUNK])
            act_ref[:, lo:lo + FF_CHUNK] = (g * jax.nn.sigmoid(g) * u).astype(BF16)
        _pages_update(qr, keys_t, values, gp, page, m_ref, l_ref, acc_ref)
    xo = x + 0.5 * _dot(act_ref[...], wout_ref[...])
    o_ref[...] = xo
    if norm_dtype is not None:
        hn_ref[...] = _rms(xo, gm_ref[...]).astype(norm_dtype)


def _ffn_pages(x, g, w_in, w_out, q_rows, cache_k_t, cache_v_rows, layer, page_table, seq0, n_seq,
               g_mix=None, norm_dtype=None):
    rows = x.shape[0]
    tm = ROW_TILE
    steps = rows // tm
    n_pages = page_table.shape[1]
    page = cache_k_t.shape[-1]
    assert rows % tm == 0 and steps % n_seq == 0
    steps_per_seq = steps // n_seq
    groups = n_pages // steps_per_seq // HOSTED_GROUP_PAGES
    assert groups * HOSTED_GROUP_PAGES * steps_per_seq == n_pages and groups % 2 == 0
    row_spec = pl.BlockSpec((tm, D_MODEL), lambda i, pt: (i, 0))
    res = lambda shape: pl.BlockSpec(shape, lambda i, pt: (0,) * len(shape), pipeline_mode=pl.Buffered(1))
    seq_spec = lambda w, off: pl.BlockSpec((1, N_SUB, w), lambda i, pt: (off + i // steps_per_seq, 0, 0))
    hbm = pl.BlockSpec(memory_space=pl.ANY)
    in_specs = [row_spec, res((1, D_MODEL)), res((D_MODEL, 2 * D_FF)), res((D_FF, D_MODEL))]
    args = [x, g.reshape(1, D_MODEL), w_in, w_out]
    out_shape = [jax.ShapeDtypeStruct((rows, D_MODEL), F32)]
    out_specs = [row_spec]
    if norm_dtype is not None:
        in_specs.append(res((1, D_MODEL)))
        args.append(g_mix.reshape(1, D_MODEL))
        out_shape.append(jax.ShapeDtypeStruct((rows, D_MODEL), norm_dtype))
        out_specs.append(row_spec)
    in_specs += [seq_spec(D_MODEL, seq0), hbm, hbm]
    args += [q_rows, cache_k_t, cache_v_rows]
    out_shape += [jax.ShapeDtypeStruct((n_seq, N_SUB, LANES), F32), jax.ShapeDtypeStruct((n_seq, N_SUB, LANES), F32),
                  jax.ShapeDtypeStruct((n_seq, N_SUB, D_MODEL), F32)]
    out_specs += [seq_spec(LANES, 0), seq_spec(LANES, 0), seq_spec(D_MODEL, 0)]
    gpp = HOSTED_GROUP_PAGES
    grid_spec = pltpu.PrefetchScalarGridSpec(
        num_scalar_prefetch=1,
        grid=(steps,),
        in_specs=in_specs,
        out_specs=out_specs,
        scratch_shapes=[pltpu.VMEM((tm, D_FF), BF16),
                        pltpu.VMEM((2, gpp, N_SUB, HEAD_DIM, page), F32),
                        pltpu.VMEM((2, gpp, page * N_HEADS, V_DIM), F32),
                        pltpu.SemaphoreType.DMA((2, 2))],
    )
    return pl.pallas_call(
        functools.partial(_ffn_pages_body, norm_dtype=norm_dtype, layer=layer, seq0=seq0,
                          steps_per_seq=steps_per_seq, groups=groups),
        grid_spec=grid_spec,
        out_shape=out_shape,
        compiler_params=_cparams("arbitrary"),
        name="ffn_pages_norm" if norm_dtype is not None else "ffn_pages",
    )(page_table, *args)


def _attn_sample_finish_body(qr_ref, kn_ref, vn_ref, m_ref, l_ref, acc_ref,
                             lq1_ref, lk1_ref, lq2_ref, lk2_ref, subln_ref, o_ref, *, lam_init):
    qr = qr_ref[0].astype(F32)
    kn = kn_ref[0].astype(F32)
    vn = vn_ref[0].astype(F32)
    s = jnp.sum(qr * kn, axis=-1, keepdims=True)
    m_old = m_ref[0][:, 0:1]
    l_old = l_ref[0][:, 0:1]
    m_new = jnp.maximum(m_old, s)
    alpha = jnp.exp(m_old - m_new)
    p = jnp.exp(s - m_new)
    l_new = l_old * alpha + p
    acc = acc_ref[0] * alpha + p.astype(BF16).astype(F32) * vn
    o = acc / l_new
    lam = _lambda_full(lq1_ref, lk1_ref, lq2_ref, lk2_ref, lam_init)
    heads = []
    for h in range(N_HEADS):
        lanes = slice(h * V_DIM, (h + 1) * V_DIM)
        heads.append(_merge(o[2 * h:2 * h + 1, lanes], o[2 * h + 1:2 * h + 2, lanes], lam, subln_ref[...], lam_init))
    o_ref[0] = jnp.concatenate(heads, axis=1).astype(BF16)


def _attn_sample_finish(q_rows, k_new, v_new, m, l, acc, lams, subln, lam_init):
    b = q_rows.shape[0]
    seq3 = lambda w: pl.BlockSpec((1, 1, w), lambda s: (s, 0, 0))
    sub3 = lambda w: pl.BlockSpec((1, N_SUB, w), lambda s: (s, 0, 0))
    small = pl.BlockSpec((1, HEAD_DIM), lambda s: (0, 0))
    out = pl.pallas_call(
        functools.partial(_attn_sample_finish_body, lam_init=lam_init),
        grid=(b,),
        in_specs=[sub3(D_MODEL), seq3(D_MODEL), seq3(D_MODEL), sub3(LANES), sub3(LANES), sub3(D_MODEL),
                  small, small, small, small, pl.BlockSpec((1, V_DIM), lambda s: (0, 0))],
        out_specs=seq3(D_MODEL),
        out_shape=jax.ShapeDtypeStruct((b, 1, D_MODEL), BF16),
        compiler_params=_cparams("parallel"),
        name="attn_sample_finish",
    )(q_rows, k_new.reshape(b, 1, D_MODEL), v_new.reshape(b, 1, D_MODEL), m, l, acc,
      *[a.astype(F32).reshape(1, HEAD_DIM) for a in lams], subln.astype(F32).reshape(1, V_DIM))
    return out.reshape(b, D_MODEL)


def _out_proj_body(x_ref, a_ref, w_ref, o_ref):
    o_ref[...] = x_ref[...] + _dot(a_ref[...], w_ref[...])


def _out_proj(x, a, w_o):
    rows = x.shape[0]
    tm = _row_tile(rows)
    row_spec = pl.BlockSpec((tm, D_MODEL), lambda i: (i, 0))
    return pl.pallas_call(
        _out_proj_body,
        grid=(rows // tm,),
        in_specs=[row_spec, row_spec, _resident((D_MODEL, D_MODEL))],
        out_specs=row_spec,
        out_shape=jax.ShapeDtypeStruct((rows, D_MODEL), F32),
        compiler_params=_cparams("parallel"),
        name="out_proj",
    )(x, a, w_o)


def _block_diag_rows(q):
    head = jnp.arange(D_MODEL) // HEAD_DIM
    keep = head[None, :] == jnp.arange(N_SUB)[:, None]
    return jnp.where(keep[None], q[:, None, :], jnp.zeros((), q.dtype))


def kernel(x_prompt, x_sample, state_s5_re, state_s5_im, cache_k, cache_v, page_table, ffn_norm, ffn_w_in, ffn_w_out, mix_norm, s5_lambda_re, s5_lambda_im, s5_log_dt, s5_b_re, s5_b_im, s5_c_re, s5_c_im, s5_d, s5_w_glu, s5_b_glu, attn_w_qkv, attn_q_norm, attn_k_norm, attn_lambda_q1, attn_lambda_k1, attn_lambda_q2, attn_lambda_k2, attn_subln, attn_w_o):
    bsz, seq, _ = x_prompt.shape
    dbs, dseq, _ = x_sample.shape
    assert dseq == 1, "the sample group decodes one token per sequence"
    assert SUBLANES % bsz == 0 and seq % S5_CHUNK == 0
    depth = ffn_norm.shape[0]
    past_len = page_table.shape[1] * cache_k.shape[2]

    xp = x_prompt.reshape(bsz * seq, D_MODEL)
    xs = x_sample.reshape(dbs * dseq, D_MODEL)
    w_in = ffn_w_in.astype(BF16)
    w_out = ffn_w_out.astype(BF16)
    f32n = ffn_norm.astype(F32)
    mixn = mix_norm.astype(F32)

    s5_re_p, s5_im_p, s5_re_s, s5_im_s = [], [], [], []
    k_p, v_p, k_s, v_s = [], [], [], []
    for i in range(depth):
        a = i // N_MIXERS
        if i % N_MIXERS == 0:
            xp, hp = _ffn(xp, f32n[i, 0], w_in[i, 0], w_out[i, 0], mixn[i], F32)
            xs, hs = _ffn(xs, f32n[i, 0], w_in[i, 0], w_out[i, 0], mixn[i], F32)
            prm = (s5_lambda_re[a], s5_lambda_im[a], s5_log_dt[a], s5_b_re[a], s5_b_im[a], s5_c_re[a], s5_c_im[a])
            w_glu = s5_w_glu[a].astype(BF16)
            yp, hr_p, hi_p = _s5_prompt(hp, _s5_chunk_tables(*prm), bsz)
            ys, hr_s, hi_s = _s5_sample(hs, state_s5_re[a], state_s5_im[a], *prm)
            xp = _s5_post(xp, yp, mixn[i], s5_d[a], w_glu, s5_b_glu[a])
            xs = _s5_post(xs, ys, mixn[i], s5_d[a], w_glu, s5_b_glu[a])
            s5_re_p.append(hr_p)
            s5_im_p.append(hi_p)
            s5_re_s.append(hr_s)
            s5_im_s.append(hi_s)
        else:
            lam_init = 0.8 - 0.6 * math.exp(-0.3 * i)
            lams = (attn_lambda_q1[a], attn_lambda_k1[a], attn_lambda_q2[a], attn_lambda_k2[a])
            w_qkv = attn_w_qkv[a].astype(BF16)
            w_o = attn_w_o[a].astype(BF16)
            cos_p, sin_p = _rope_tables(jnp.arange(seq))
            cos_s, sin_s = _rope_tables(jnp.tile(past_len + jnp.arange(dseq), dbs))
            xs, hs = _ffn(xs, f32n[i, 0], w_in[i, 0], w_out[i, 0], mixn[i], BF16)
            qs, ksf, ksb, vsf, vsb = _qkv(hs, w_qkv, attn_q_norm[a], attn_k_norm[a], cos_s, sin_s)
            q_rows = _block_diag_rows(qs)
            ck_t = jnp.transpose(cache_k, (0, 1, 3, 4, 2))
            cv_rows = cache_v.reshape(cache_v.shape[:2] + (cache_v.shape[2] * N_HEADS, V_DIM))
            first = dbs // 2
            xp, hp, m0, l0, acc0 = _ffn_pages(xp, f32n[i, 0], w_in[i, 0], w_out[i, 0], q_rows, ck_t, cv_rows, a,
                                              page_table, 0, first, mixn[i], BF16)
            qp, kpf, kpb, vpf, vpb = _qkv(hp, w_qkv, attn_q_norm[a], attn_k_norm[a], cos_p, sin_p, seq_minor_k=True)
            op = _attn_prompt(qp, kpb, vpb, lams, attn_subln[a], lam_init, bsz)
            xp = _out_proj(xp, op, w_o)
            xp, m1, l1, acc1 = _ffn_pages(xp, f32n[i, 1], w_in[i, 1], w_out[i, 1], q_rows, ck_t, cv_rows, a,
                                          page_table, first, dbs - first)
            m, l, acc = (jnp.concatenate(parts, axis=0) for parts in ((m0, m1), (l0, l1), (acc0, acc1)))
            os_ = _attn_sample_finish(q_rows, ksb, vsb, m, l, acc, lams, attn_subln[a], lam_init)
            xs = _out_proj(xs, os_, w_o)
            xs = _ffn(xs, f32n[i, 1], w_in[i, 1], w_out[i, 1])
            k_p.append(kpf.transpose(0, 3, 1, 2))
            v_p.append(vpf.reshape(bsz, seq, N_HEADS, V_DIM))
            k_s.append(ksf.reshape(dbs, dseq, N_SUB, HEAD_DIM))
            v_s.append(vsf.reshape(dbs, dseq, N_HEADS, V_DIM))
            continue
        xp = _ffn(xp, f32n[i, 1], w_in[i, 1], w_out[i, 1])
        xs = _ffn(xs, f32n[i, 1], w_in[i, 1], w_out[i, 1])
    return (xp.reshape(bsz, seq, D_MODEL), xs.reshape(dbs, dseq, D_MODEL),
            jnp.stack(s5_re_p), jnp.stack(s5_im_p), jnp.stack(s5_re_s), jnp.stack(s5_im_s),
            jnp.stack(k_p), jnp.stack(v_p), jnp.stack(k_s), jnp.stack(v_s))
```

```python
import functools
import math

import jax
import jax.numpy as jnp
from jax import lax
from jax.experimental import pallas as pl
from jax.experimental.pallas import tpu as pltpu

F32 = jnp.float32
BF16 = jnp.bfloat16

D_MODEL = 1024
D_FF = 2816
S5_GROUP = 16
S5_GROUPS = D_MODEL // S5_GROUP
S5_STATE = 64
S5_CHUNK = 16
N_HEADS = 8
N_SUB = 2 * N_HEADS
HEAD_DIM = D_MODEL // N_SUB
V_DIM = 2 * HEAD_DIM
ATTN_SCALE = HEAD_DIM ** -0.5
ROPE_THETA = 10000.0
EPS = 1e-6
NEG_INF = -1e30
N_MIXERS = 2

LANES = 128
SUBLANES = 8
MXU_TILE = 256
FF_CHUNK = MXU_TILE
ROW_TILE = 512
ATTN_Q_TILE = 1024
ATTN_K_TILE = 1024
HOSTED_GROUP_PAGES = 4
HOSTED_RING = 4
LANE_GROUPS = LANES // S5_GROUP
S5_SEQS_PER_STEP = 2
S5_RELAYOUT_ROWS = 64
VMEM_LIMIT = 48 * 1024 * 1024
VMEM_LIMIT_HOSTED = 56 * 1024 * 1024


def _cparams(*sem, vmem=None):
    return pltpu.CompilerParams(dimension_semantics=sem, vmem_limit_bytes=vmem or VMEM_LIMIT)


def _resident(shape):
    zeros = (0,) * len(shape)
    return pl.BlockSpec(shape, lambda *_: zeros, pipeline_mode=pl.Buffered(1))


def _rms(x, g):
    return x * lax.rsqrt(jnp.mean(x * x, axis=-1, keepdims=True) + EPS) * g


def _dot(a, b):
    return jnp.dot(a, b, preferred_element_type=F32)


def _dot_nt(a, b):
    return lax.dot_general(a, b, (((1,), (1,)), ((), ())), preferred_element_type=F32)


def _row_tile(rows):
    return ROW_TILE if rows % ROW_TILE == 0 else rows


def _ffn_body(x_ref, g_ref, win_ref, wout_ref, *rest, norm_dtype):
    if norm_dtype is None:
        o_ref, act_ref = rest
    else:
        gm_ref, o_ref, hn_ref, act_ref = rest
    x = x_ref[...]
    xn = _rms(x, g_ref[...]).astype(BF16)
    for c in range(D_FF // FF_CHUNK):
        lo = c * FF_CHUNK
        g = _dot(xn, win_ref[:, lo:lo + FF_CHUNK])
        u = _dot(xn, win_ref[:, D_FF + lo:D_FF + lo + FF_CHUNK])
        act_ref[:, lo:lo + FF_CHUNK] = (g * jax.nn.sigmoid(g) * u).astype(BF16)
    xo = x + 0.5 * _dot(act_ref[...], wout_ref[...])
    o_ref[...] = xo
    if norm_dtype is not None:
        hn_ref[...] = _rms(xo, gm_ref[...]).astype(norm_dtype)


def _ffn(x, g, w_in, w_out, g_mix=None, norm_dtype=None):
    rows = x.shape[0]
    tm = _row_tile(rows)
    row_spec = pl.BlockSpec((tm, D_MODEL), lambda i: (i, 0))
    in_specs = [row_spec, _resident((1, D_MODEL)), _resident((D_MODEL, 2 * D_FF)), _resident((D_FF, D_MODEL))]
    args = [x, g.reshape(1, D_MODEL), w_in, w_out]
    out_shape = [jax.ShapeDtypeStruct((rows, D_MODEL), F32)]
    out_specs = [row_spec]
    if norm_dtype is not None:
        in_specs.append(_resident((1, D_MODEL)))
        args.append(g_mix.reshape(1, D_MODEL))
        out_shape.append(jax.ShapeDtypeStruct((rows, D_MODEL), norm_dtype))
        out_specs.append(row_spec)
    res = pl.pallas_call(
        functools.partial(_ffn_body, norm_dtype=norm_dtype),
        grid=(rows // tm,),
        in_specs=in_specs,
        out_specs=out_specs,
        out_shape=out_shape,
        scratch_shapes=[pltpu.VMEM((tm, D_FF), BF16)],
        compiler_params=_cparams("parallel"),
        name="ffn_norm" if norm_dtype is not None else "ffn",
    )(*args)
    return res if norm_dtype is not None else res[0]


def _s5_discretise(lam_re, lam_im, log_dt, b_re, b_im):
    dt = jnp.exp(log_dt.astype(F32))[:, None]
    lr = lam_re.astype(F32)
    li = lam_im.astype(F32)
    mag = jnp.exp(lr * dt)
    a_re = mag * jnp.cos(li * dt)
    a_im = mag * jnp.sin(li * dt)
    den = lr * lr + li * li
    nr = a_re - 1.0
    z_re = (nr * lr + a_im * li) / den
    z_im = (a_im * lr - nr * li) / den
    br = b_re.astype(F32)
    bi = b_im.astype(F32)
    bb_re = z_re[..., None] * br - z_im[..., None] * bi
    bb_im = z_re[..., None] * bi + z_im[..., None] * br
    return a_re, a_im, bb_re, bb_im, lr * dt, li * dt


def _s5_chunk_tables(lam_re, lam_im, log_dt, b_re, b_im, c_re, c_im):
    L, G, N, C = S5_CHUNK, S5_GROUPS, S5_STATE, S5_GROUP
    _, _, bb_re, bb_im, ldr, ldi = _s5_discretise(lam_re, lam_im, log_dt, b_re, b_im)
    tau = jnp.arange(L + 1, dtype=F32)[:, None, None]
    mag = jnp.exp(ldr[None] * tau)
    p_re = mag * jnp.cos(ldi[None] * tau)
    p_im = mag * jnp.sin(ldi[None] * tau)
    cr = c_re.astype(F32)
    ci = c_im.astype(F32)
    hi = lax.Precision.HIGHEST
    ca_re = cr[None] * p_re[:, :, None, :] - ci[None] * p_im[:, :, None, :]
    ca_im = cr[None] * p_im[:, :, None, :] + ci[None] * p_re[:, :, None, :]
    ca_rows = jnp.concatenate([ca_re[:L], ca_im[:L]], axis=-1).transpose(1, 3, 0, 2).reshape(G, 2 * N, L * C)
    bb_rows = jnp.concatenate([bb_re, -bb_im], axis=1)
    resp = jnp.einsum('gki,gkx->gix', bb_rows, ca_rows, precision=hi)
    toep = jnp.stack([jnp.pad(resp[:, :, :(L - s) * C], ((0, 0), (0, 0), (s * C, 0))) for s in range(L)], axis=1)
    toep = toep.reshape(G, L * C, L * C)
    pw_re = p_re[L - 1 - jnp.arange(L)]
    pw_im = p_im[L - 1 - jnp.arange(L)]
    w_re = pw_re[..., None] * bb_re[None] - pw_im[..., None] * bb_im[None]
    w_im = pw_re[..., None] * bb_im[None] + pw_im[..., None] * bb_re[None]
    w_re = w_re.transpose(1, 0, 3, 2).reshape(G, L * C, N)
    w_im = w_im.transpose(1, 0, 3, 2).reshape(G, L * C, N)
    w_in = jnp.concatenate([w_re, w_im, w_im, w_re], axis=-1)
    v_re = ca_re[1:].transpose(1, 3, 0, 2).reshape(G, N, L * C)
    v_im = -ca_im[1:].transpose(1, 3, 0, 2).reshape(G, N, L * C)
    v_out = jnp.concatenate([v_re, v_im], axis=1)
    al_re, al_im = p_re[L], p_im[L]
    decay = jnp.stack([jnp.concatenate([al_re, al_re], -1),
                       jnp.concatenate([-al_im, al_im], -1),
                       jnp.concatenate([al_im, -al_im], -1)], axis=1)
    return toep.astype(BF16), w_in.astype(BF16), v_out.astype(BF16), decay


def _block_transpose(xs):
    n = len(xs)
    block = lax.broadcasted_iota(jnp.int32, xs[0].shape, 1) // S5_GROUP
    d = n // 2
    while d >= 1:
        upper = (block // d) % 2 == 1
        nxt = list(xs)
        for i in range(n):
            if i & d == 0:
                a, b = xs[i], xs[i | d]
                nxt[i] = jnp.where(upper, pltpu.roll(b, d * S5_GROUP, 1), a)
                nxt[i | d] = jnp.where(upper, b, pltpu.roll(a, LANES - d * S5_GROUP, 1))
        xs = nxt
        d //= 2
    return xs


def _s5_prompt_body(hn_ref, toep_ref, win_ref, vout_ref, dec_ref, y_ref, hfin_ref, u_scr, sy_scr, hp_scr, *, sps):
    lg = LANE_GROUPS
    rows = u_scr.shape[2]
    nch = rows // sps
    n2 = 2 * S5_STATE
    half = S5_CHUNK // 2
    rb = S5_RELAYOUT_ROWS
    blocks_per_seq = nch // rb

    def chunk_rows(blk):
        seq = blk // blocks_per_seq
        c0 = (blk % blocks_per_seq) * rb
        return c0 * sps + seq, (seq * nch + c0) * S5_CHUNK

    def gather(blk, _):
        r0, t0 = chunk_rows(blk)
        for h in range(2):
            xs = [hn_ref[pl.ds(t0 + h * half + tt, rb, stride=S5_CHUNK), :] for tt in range(half)]
            for g, y in enumerate(_block_transpose(xs)):
                u_scr[g, h, pl.ds(r0, rb, stride=sps), :] = y
        return _

    lax.fori_loop(0, sps * blocks_per_seq, gather, 0)

    def chunk_inputs(g):
        return jnp.concatenate([u_scr[g, 0], u_scr[g, 1]], axis=1).astype(BF16)

    for g in range(lg):
        s = _dot(chunk_inputs(g), win_ref[g])
        sy_scr[g, 0] = s[:, :n2]
        sy_scr[g, 1] = s[:, n2:]
    decs = [dec_ref[g] for g in range(lg)]
    cpv = SUBLANES // sps

    def step(i, carry):
        r0 = pl.multiple_of(i * SUBLANES, SUBLANES)
        new = []
        for g in range(lg):
            h, hs = carry[g]
            a1 = decs[g][0:1, :]
            a2 = decs[g][1:2, :]
            a2s = decs[g][2:3, :]
            slab = sy_scr[g, 0, pl.ds(r0, SUBLANES), :]
            slab_s = sy_scr[g, 1, pl.ds(r0, SUBLANES), :]
            prev = []
            for c in range(cpv):
                prev.append(h)
                rs = slice(c * sps, (c + 1) * sps)
                h, hs = (a1 * h + a2 * hs + slab[rs], a1 * hs + a2s * h + slab_s[rs])
            hp_scr[g, pl.ds(r0, SUBLANES), :] = jnp.concatenate(prev, axis=0)
            new.append((h, hs))
        return tuple(new)

    zero = jnp.zeros((sps, n2), F32)
    fin = lax.fori_loop(0, rows // SUBLANES, step, tuple((zero, zero) for _ in range(lg)))
    for g in range(lg):
        hfin_ref[0, g] = fin[g][0]
        y = _dot(chunk_inputs(g), toep_ref[g]) + _dot(hp_scr[g].astype(BF16), vout_ref[g])
        sy_scr[g, 0] = y[:, :n2]
        sy_scr[g, 1] = y[:, n2:]

    def scatter(blk, _):
        r0, t0 = chunk_rows(blk)
        for h in range(2):
            ys = [sy_scr[g, h, pl.ds(r0, rb, stride=sps), :] for g in range(lg)]
            for tt, x in enumerate(_block_transpose(ys)):
                y_ref[pl.ds(t0 + h * half + tt, rb, stride=S5_CHUNK), :] = x
        return _

    lax.fori_loop(0, sps * blocks_per_seq, scatter, 0)


def _s5_prompt(hn, tables, bsz):
    toep, w_in, v_out, decay = tables
    G, L, N = S5_GROUPS, S5_CHUNK, S5_STATE
    t = hn.shape[0] // bsz
    sps = S5_SEQS_PER_STEP
    assert bsz % sps == 0 and SUBLANES % sps == 0 and (t // L) % S5_RELAYOUT_ROWS == 0
    rows = (t // L) * sps
    lg = LANE_GROUPS
    lc = L * S5_GROUP
    tok_spec = pl.BlockSpec((sps * t, LANES), lambda j, s: (s, j))
    gspec = lambda *shape: pl.BlockSpec((lg,) + shape, lambda j, s: (j,) + (0,) * len(shape))
    y, hfin = pl.pallas_call(
        functools.partial(_s5_prompt_body, sps=sps),
        grid=(G // lg, bsz // sps),
        in_specs=[tok_spec, gspec(lc, lc), gspec(lc, lc), gspec(2 * N, lc), gspec(3, 2 * N)],
        out_specs=[tok_spec, pl.BlockSpec((1, lg, sps, 2 * N), lambda j, s: (s, j, 0, 0))],
        out_shape=[jax.ShapeDtypeStruct(hn.shape, F32), jax.ShapeDtypeStruct((bsz // sps, G, sps, 2 * N), F32)],
        scratch_shapes=[pltpu.VMEM((lg, 2, rows, LANES), F32), pltpu.VMEM((lg, 2, rows, LANES), F32),
                        pltpu.VMEM((lg, rows, 2 * N), F32)],
        compiler_params=_cparams("parallel", "parallel"),
        name="s5_prompt",
    )(hn, toep, w_in, v_out, decay)
    hfin = hfin.transpose(0, 2, 1, 3).reshape(bsz, G, 2 * N)
    return y, hfin[:, :, :N], hfin[:, :, N:]


def _s5_sample_body(u_ref, bbr_ref, bbi_ref, ar_ref, ai_ref, h0r_ref, h0i_ref, cr_ref, ci_ref,
                    y_ref, hr_ref, hi_ref):
    hi = lax.Precision.HIGHEST
    u = u_ref[...]
    bu_re = jnp.einsum('gbc,gnc->gbn', u, bbr_ref[...], precision=hi, preferred_element_type=F32)
    bu_im = jnp.einsum('gbc,gnc->gbn', u, bbi_ref[...], precision=hi, preferred_element_type=F32)
    ar, ai = ar_ref[...], ai_ref[...]
    h0r, h0i = h0r_ref[...], h0i_ref[...]
    hr = bu_re + ar * h0r - ai * h0i
    hi_ = bu_im + ar * h0i + ai * h0r
    hr_ref[...] = hr
    hi_ref[...] = hi_
    y_ref[...] = (jnp.einsum('gbn,gcn->gbc', hr, cr_ref[...], precision=hi, preferred_element_type=F32)
                  - jnp.einsum('gbn,gcn->gbc', hi_, ci_ref[...], precision=hi, preferred_element_type=F32))


def _s5_sample(hs, h0_re, h0_im, lam_re, lam_im, log_dt, b_re, b_im, c_re, c_im):
    G, C, N = S5_GROUPS, S5_GROUP, S5_STATE
    b = hs.shape[0]
    a_re, a_im, bb_re, bb_im, _, _ = _s5_discretise(lam_re, lam_im, log_dt, b_re, b_im)
    u = hs.reshape(b, G, C).transpose(1, 0, 2)
    args = [u, bb_re, bb_im, a_re[:, None, :], a_im[:, None, :],
            h0_re.astype(F32).transpose(1, 0, 2), h0_im.astype(F32).transpose(1, 0, 2),
            c_re.astype(F32), c_im.astype(F32)]
    full = lambda a: pl.BlockSpec(a.shape, lambda i: (0,) * a.ndim)
    outs = [jax.ShapeDtypeStruct((G, b, C), F32), jax.ShapeDtypeStruct((G, b, N), F32),
            jax.ShapeDtypeStruct((G, b, N), F32)]
    y, hr, hi = pl.pallas_call(
        _s5_sample_body,
        grid=(1,),
        in_specs=[full(a) for a in args],
        out_specs=[full(o) for o in outs],
        out_shape=outs,
        compiler_params=_cparams("arbitrary"),
        name="s5_sample",
    )(*args)
    return y.transpose(1, 0, 2).reshape(b, D_MODEL), hr.transpose(1, 0, 2), hi.transpose(1, 0, 2)


def _s5_post_body(x_ref, y_ref, gm_ref, d_ref, w_ref, b_ref, o_ref):
    x = x_ref[...]
    hp = _rms(x, gm_ref[...])
    z = jax.nn.gelu(y_ref[...] + d_ref[...] * hp)
    zz = _dot(z.astype(BF16), w_ref[...]) + b_ref[...]
    o_ref[...] = x + zz[:, :D_MODEL] * jax.nn.sigmoid(zz[:, D_MODEL:])


def _s5_post(x, y, g_mix, d, w_glu, b_glu):
    rows = x.shape[0]
    tm = _row_tile(rows)
    row_spec = pl.BlockSpec((tm, D_MODEL), lambda i: (i, 0))
    return pl.pallas_call(
        _s5_post_body,
        grid=(rows // tm,),
        in_specs=[row_spec, row_spec, _resident((1, D_MODEL)), _resident((1, D_MODEL)),
                  _resident((D_MODEL, 2 * D_MODEL)), _resident((1, 2 * D_MODEL))],
        out_specs=row_spec,
        out_shape=jax.ShapeDtypeStruct((rows, D_MODEL), F32),
        compiler_params=_cparams("parallel"),
        name="s5_post",
    )(x, y, g_mix.reshape(1, D_MODEL), d.astype(F32).reshape(1, D_MODEL), w_glu, b_glu.astype(F32).reshape(1, -1))


def _rope_tables(pos):
    half = HEAD_DIM // 2
    inv = ROPE_THETA ** (-jnp.arange(half, dtype=F32) / half)
    ang = pos.astype(F32)[:, None] * inv[None, :]
    cos = jnp.cos(ang)
    sin = jnp.sin(ang)
    cos_t = jnp.concatenate([cos, cos, cos, cos], axis=-1)
    sin_t = jnp.concatenate([-sin, sin, -sin, sin], axis=-1)
    return cos_t, sin_t


def _head_sum_matrix():
    head = jnp.arange(D_MODEL) // HEAD_DIM
    return (head[:, None] == head[None, :]).astype(BF16)


def _qkv_body(hn_ref, w_ref, qg_ref, kg_ref, hsum_ref, cos_ref, sin_ref,
              q_ref, kf_ref, kb_ref, vf_ref, vb_ref, *, seq_minor_k):
    hn = hn_ref[...]
    cos = cos_ref[...]
    sin = sin_ref[...]
    lane = lax.broadcasted_iota(jnp.int32, cos.shape, 1)
    low_half = (lane % HEAD_DIM) < (HEAD_DIM // 2)

    def norm_rope(x, g):
        ss = _dot((x * x).astype(BF16), hsum_ref[...])
        y = x * lax.rsqrt(ss * (1.0 / HEAD_DIM) + EPS) * g
        outs = []
        for j in range(D_MODEL // LANES):
            t = y[:, j * LANES:(j + 1) * LANES]
            below = pltpu.roll(t, HEAD_DIM // 2, 1)
            above = pltpu.roll(t, LANES - HEAD_DIM // 2, 1)
            outs.append(t * cos + jnp.where(low_half, above, below) * sin)
        return jnp.concatenate(outs, axis=1)

    q = norm_rope(_dot(hn, w_ref[:, 0:D_MODEL]), qg_ref[...])
    q_ref[...] = (q * ATTN_SCALE).astype(BF16)
    k = norm_rope(_dot(hn, w_ref[:, D_MODEL:2 * D_MODEL]), kg_ref[...])
    if seq_minor_k:
        kf_ref[0] = k.T.reshape(kf_ref.shape[1:])
    else:
        kf_ref[...] = k
    kb_ref[...] = k.astype(BF16)
    v = _dot(hn, w_ref[:, 2 * D_MODEL:3 * D_MODEL])
    vf_ref[...] = v
    vb_ref[...] = v.astype(BF16)


def _qkv(hn, w_qkv, q_norm, k_norm, cos_t, sin_t, seq_minor_k=False):
    rows = hn.shape[0]
    tm = _row_tile(rows)
    tab_blocks = cos_t.shape[0] // tm
    row_spec = pl.BlockSpec((tm, D_MODEL), lambda i: (i, 0))
    tab_spec = pl.BlockSpec((tm, LANES), lambda i: (i % tab_blocks, 0))
    gain = lambda g: jnp.tile(g.astype(F32), N_SUB).reshape(1, D_MODEL)
    outs = [jax.ShapeDtypeStruct((rows, D_MODEL), dt) for dt in (BF16, F32, BF16, F32, BF16)]
    out_specs = [row_spec] * 5
    if seq_minor_k:
        period = cos_t.shape[0]
        outs[1] = jax.ShapeDtypeStruct((rows // period, N_SUB, HEAD_DIM, period), F32)
        out_specs[1] = pl.BlockSpec((1, N_SUB, HEAD_DIM, tm), lambda i: (i // tab_blocks, 0, 0, i % tab_blocks))
    return pl.pallas_call(
        functools.partial(_qkv_body, seq_minor_k=seq_minor_k),
        grid=(rows // tm,),
        in_specs=[row_spec, _resident((D_MODEL, 3 * D_MODEL)), _resident((1, D_MODEL)), _resident((1, D_MODEL)),
                  _resident((D_MODEL, D_MODEL)), tab_spec, tab_spec],
        out_specs=out_specs,
        out_shape=outs,
        compiler_params=_cparams("parallel"),
        name="qkv",
    )(hn, w_qkv, gain(q_norm), gain(k_norm), _head_sum_matrix(), cos_t, sin_t)


def _lambda_full(lq1_ref, lk1_ref, lq2_ref, lk2_ref, lam_init):
    s1 = jnp.sum(lq1_ref[...] * lk1_ref[...], axis=-1, keepdims=True)
    s2 = jnp.sum(lq2_ref[...] * lk2_ref[...], axis=-1, keepdims=True)
    return jnp.exp(s1) - jnp.exp(s2) + lam_init


def _merge(o1, o2, lam, subln, lam_init):
    out = o1 - lam * o2
    return _rms(out, subln) * (1.0 - lam_init)


def _attn_prompt_body(q_ref, k_ref, v_ref, lq1_ref, lk1_ref, lq2_ref, lk2_ref, subln_ref, o_ref, *, lam_init, tk, nq):
    tq = q_ref.shape[0]
    qi = pl.program_id(2)
    kt_per_q = tq // tk
    qt = q_ref[...].astype(F32).T.astype(BF16)
    dim = lax.broadcasted_iota(jnp.int32, qt.shape, 0)
    zero = jnp.zeros_like(qt)
    qst = jnp.concatenate([jnp.where(dim < HEAD_DIM, qt, zero), jnp.where(dim >= HEAD_DIM, qt, zero)], axis=1)

    span = MXU_TILE
    n_span = 2 * tq // span

    def visible_keys(c, diagonal):
        if diagonal is None:
            return tk
        return max(0, min(tk, (c * span) % tq + span - diagonal * tk))

    def scores(j, diagonal):
        out = []
        for c in range(n_span):
            nk = visible_keys(c, diagonal)
            out.append(_dot(k_ref[pl.ds(j * tk, nk), :], qst[:, c * span:(c + 1) * span]) if nk else None)
        return out

    def absorb(j, tile_scores, carry, diagonal):
        new = []
        for c, (s, (m, l, acc)) in enumerate(zip(tile_scores, carry)):
            if s is None:
                new.append((m, l, acc))
                continue
            if diagonal is not None:
                key = lax.broadcasted_iota(jnp.int32, s.shape, 0) + diagonal * tk
                qry = lax.broadcasted_iota(jnp.int32, s.shape, 1) + (c * span) % tq
                s = jnp.where(key <= qry, s, NEG_INF)
            m_new = jnp.maximum(m, jnp.max(s, axis=0, keepdims=True))
            alpha = jnp.exp(m - m_new)
            p = jnp.exp(s - m_new)
            l_new = l * alpha + jnp.sum(p, axis=0, keepdims=True)
            vj = v_ref[pl.ds(j * tk, s.shape[0]), :]
            pv = lax.dot_general(vj, p.astype(BF16), (((0,), (0,)), ((), ())), preferred_element_type=F32)
            new.append((m_new, l_new, acc * alpha + pv))
        return new

    def run(n_tiles):
        carry = [(jnp.full((1, span), NEG_INF, F32), jnp.zeros((1, span), F32), jnp.zeros((V_DIM, span), F32))
                 for _ in range(n_span)]
        def diagonal(j):
            d = j - (n_tiles - kt_per_q)
            return d if d >= 0 else None

        tile_scores = scores(0, diagonal(0))
        for j in range(n_tiles):
            next_scores = scores(j + 1, diagonal(j + 1)) if j + 1 < n_tiles else None
            carry = absorb(j, tile_scores, carry, diagonal(j))
            tile_scores = next_scores
        o = jnp.concatenate([acc / l for _, l, acc in carry], axis=1)
        lam = _lambda_full(lq1_ref, lk1_ref, lq2_ref, lk2_ref, lam_init)
        out = o[:, :tq] - lam * o[:, tq:]
        out = out * lax.rsqrt(jnp.mean(out * out, axis=0, keepdims=True) + EPS) * subln_ref[...] * (1.0 - lam_init)
        o_ref[...] = out.T.astype(BF16)

    for n in range(nq):
        pl.when(qi == n)(functools.partial(run, (n + 1) * kt_per_q))


def _attn_prompt(q, k, v, lams, subln, lam_init, bsz):
    rows = q.shape[0]
    t = rows // bsz
    tq = ATTN_Q_TILE if t % ATTN_Q_TILE == 0 else t
    tk = ATTN_K_TILE if tq % ATTN_K_TILE == 0 else tq
    nq = t // tq
    small = pl.BlockSpec((1, HEAD_DIM), lambda b, h, i: (0, 0))
    return pl.pallas_call(
        functools.partial(_attn_prompt_body, lam_init=lam_init, tk=tk, nq=nq),
        grid=(bsz, N_HEADS, nq),
        in_specs=[pl.BlockSpec((tq, V_DIM), lambda b, h, i: (b * nq + i, h)),
                  pl.BlockSpec((t, V_DIM), lambda b, h, i: (b, h)),
                  pl.BlockSpec((t, V_DIM), lambda b, h, i: (b, h)),
                  small, small, small, small,
                  pl.BlockSpec((V_DIM, 1), lambda b, h, i: (0, 0))],
        out_specs=pl.BlockSpec((tq, V_DIM), lambda b, h, i: (b * nq + i, h)),
        out_shape=jax.ShapeDtypeStruct((rows, D_MODEL), BF16),
        compiler_params=_cparams("parallel", "parallel", "arbitrary"),
        name="attn_prompt",
    )(q, k, v, *[a.astype(F32).reshape(1, HEAD_DIM) for a in lams], subln.astype(F32).reshape(V_DIM, 1))


def _pages_update(qr, keys_t, values, n, page, m_ref, l_ref, acc_ref):
    s = jnp.concatenate([_dot(qr, keys_t(i)) for i in range(n)], axis=1)
    m_old = m_ref[0][:, 0:1]
    l_old = l_ref[0][:, 0:1]
    m_new = jnp.maximum(m_old, jnp.max(s, axis=-1, keepdims=True))
    alpha = jnp.exp(m_old - m_new)
    p = jnp.exp(s - m_new)
    l_new = l_old * alpha + jnp.sum(p, axis=-1, keepdims=True)
    pb = p.astype(BF16)
    pv = _dot(pb[:, 0:page], values(0))
    for i in range(1, n):
        pv = pv + _dot(pb[:, i * page:(i + 1) * page], values(i))
    acc_ref[0] = acc_ref[0] * alpha + pv
    m_ref[0] = jnp.broadcast_to(m_new, m_ref.shape[1:])
    l_ref[0] = jnp.broadcast_to(l_new, l_ref.shape[1:])


def _ffn_pages_body(pt_ref, x_ref, g_ref, win_ref, wout_ref, *rest, norm_dtype, layer, seq0, steps_per_seq, groups):
    if norm_dtype is None:
        qr_ref, ck_ref, cv_ref, o_ref, m_ref, l_ref, acc_ref, act_ref, kbuf, vbuf, sem = rest
    else:
        gm_ref, qr_ref, ck_ref, cv_ref, o_ref, hn_ref, m_ref, l_ref, acc_ref, act_ref, kbuf, vbuf, sem = rest
    i = pl.program_id(0)
    last = pl.num_programs(0) - 1
    gp = HOSTED_GROUP_PAGES
    page = kbuf.shape[-1]

    def copies(step, grp, slot):
        seq = seq0 + step // steps_per_seq
        first = (step % steps_per_seq) * (groups * gp) + grp * gp
        out = []
        for pg in range(gp):
            phys = pt_ref[seq, first + pg]
            out.append(pltpu.make_async_copy(ck_ref.at[layer, phys], kbuf.at[slot, pg], sem.at[slot, 0]))
            out.append(pltpu.make_async_copy(cv_ref.at[layer, phys], vbuf.at[slot, pg], sem.at[slot, 1]))
        return out

    def start(step, grp, slot):
        for c in copies(step, grp, slot):
            c.start()

    ring = kbuf.shape[0]
    ahead = ring - 1

    @pl.when(i == 0)
    def _():
        for grp in range(ahead):
            start(i, grp, grp)

    @pl.when(i % steps_per_seq == 0)
    def _():
        m_ref[...] = jnp.full(m_ref.shape, NEG_INF, F32)
        l_ref[...] = jnp.zeros(l_ref.shape, F32)
        acc_ref[...] = jnp.zeros(acc_ref.shape, F32)

    x = x_ref[...]
    xn = _rms(x, g_ref[...]).astype(BF16)
    qr = qr_ref[0]
    n_chunks = D_FF // FF_CHUNK
    for grp in range(groups):
        slot = grp % ring
        nxt = grp + ahead
        if nxt < groups:
            start(i, nxt, nxt % ring)
        else:
            @pl.when(i < last)
            def _(nxt=nxt):
                start(i + 1, nxt - groups, nxt % ring)
        for c in copies(i, grp, slot):
            c.wait()

        def keys_t(pg, slot=slot):
            return kbuf[slot, pg].reshape(D_MODEL, page).astype(BF16)

        def values(pg, slot=slot):
            heads = [vbuf[slot, pg, pl.ds(h, page, stride=N_HEADS), :] for h in range(N_HEADS)]
            return jnp.concatenate(heads, axis=1).astype(BF16)

        for c in range(grp * n_chunks // groups, (grp + 1) * n_chunks // groups):
            lo = c * FF_CHUNK
            g = _dot(xn, win_ref[:, lo:lo + FF_CHUNK])
            u = _dot(xn, win_ref[:, D_FF + lo:D_FF + lo + FF_CHUNK])
            act_ref[:, lo:lo + FF_CHUNK] = (g * jax.nn.sigmoid(g) * u).astype(BF16)
        _pages_update(qr, keys_t, values, gp, page, m_ref, l_ref, acc_ref)
    xo = x + 0.5 * _dot(act_ref[...], wout_ref[...])
    o_ref[...] = xo
    if norm_dtype is not None:
        hn_ref[...] = _rms(xo, gm_ref[...]).astype(norm_dtype)


def _ffn_pages(x, g, w_in, w_out, q_rows, cache_k_t, cache_v_rows, layer, page_table, seq0, n_seq,
               g_mix=None, norm_dtype=None):
    rows = x.shape[0]
    tm = ROW_TILE
    steps = rows // tm
    n_pages = page_table.shape[1]
    page = cache_k_t.shape[-1]
    assert rows % tm == 0 and steps % n_seq == 0
    steps_per_seq = steps // n_seq
    groups = n_pages // steps_per_seq // HOSTED_GROUP_PAGES
    assert groups * HOSTED_GROUP_PAGES * steps_per_seq == n_pages and groups % HOSTED_RING == 0
    row_spec = pl.BlockSpec((tm, D_MODEL), lambda i, pt: (i, 0))
    res = lambda shape: pl.BlockSpec(shape, lambda i, pt: (0,) * len(shape), pipeline_mode=pl.Buffered(1))
    seq_spec = lambda w, off: pl.BlockSpec((1, N_SUB, w), lambda i, pt: (off + i // steps_per_seq, 0, 0))
    hbm = pl.BlockSpec(memory_space=pl.ANY)
    in_specs = [row_spec, res((1, D_MODEL)), res((D_MODEL, 2 * D_FF)), res((D_FF, D_MODEL))]
    args = [x, g.reshape(1, D_MODEL), w_in, w_out]
    out_shape = [jax.ShapeDtypeStruct((rows, D_MODEL), F32)]
    out_specs = [row_spec]
    if norm_dtype is not None:
        in_specs.append(res((1, D_MODEL)))
        args.append(g_mix.reshape(1, D_MODEL))
        out_shape.append(jax.ShapeDtypeStruct((rows, D_MODEL), norm_dtype))
        out_specs.append(row_spec)
    in_specs += [seq_spec(D_MODEL, seq0), hbm, hbm]
    args += [q_rows, cache_k_t, cache_v_rows]
    out_shape += [jax.ShapeDtypeStruct((n_seq, N_SUB, LANES), F32), jax.ShapeDtypeStruct((n_seq, N_SUB, LANES), F32),
                  jax.ShapeDtypeStruct((n_seq, N_SUB, D_MODEL), F32)]
    out_specs += [seq_spec(LANES, 0), seq_spec(LANES, 0), seq_spec(D_MODEL, 0)]
    gpp = HOSTED_GROUP_PAGES
    grid_spec = pltpu.PrefetchScalarGridSpec(
        num_scalar_prefetch=1,
        grid=(steps,),
        in_specs=in_specs,
        out_specs=out_specs,
        scratch_shapes=[pltpu.VMEM((tm, D_FF), BF16),
                        pltpu.VMEM((HOSTED_RING, gpp, N_SUB, HEAD_DIM, page), F32),
                        pltpu.VMEM((HOSTED_RING, gpp, page * N_HEADS, V_DIM), F32),
                        pltpu.SemaphoreType.DMA((HOSTED_RING, 2))],
    )
    return pl.pallas_call(
        functools.partial(_ffn_pages_body, norm_dtype=norm_dtype, layer=layer, seq0=seq0,
                          steps_per_seq=steps_per_seq, groups=groups),
        grid_spec=grid_spec,
        out_shape=out_shape,
        compiler_params=_cparams("arbitrary", vmem=VMEM_LIMIT_HOSTED),
        name="ffn_pages_norm" if norm_dtype is not None else "ffn_pages",
    )(page_table, *args)


def _attn_sample_finish_body(qr_ref, kn_ref, vn_ref, m_ref, l_ref, acc_ref,
                             lq1_ref, lk1_ref, lq2_ref, lk2_ref, subln_ref, o_ref, *, lam_init):
    qr = qr_ref[0].astype(F32)
    kn = kn_ref[0].astype(F32)
    vn = vn_ref[0].astype(F32)
    s = jnp.sum(qr * kn, axis=-1, keepdims=True)
    m_old = m_ref[0][:, 0:1]
    l_old = l_ref[0][:, 0:1]
    m_new = jnp.maximum(m_old, s)
    alpha = jnp.exp(m_old - m_new)
    p = jnp.exp(s - m_new)
    l_new = l_old * alpha + p
    acc = acc_ref[0] * alpha + p.astype(BF16).astype(F32) * vn
    o = acc / l_new
    lam = _lambda_full(lq1_ref, lk1_ref, lq2_ref, lk2_ref, lam_init)
    heads = []
    for h in range(N_HEADS):
        lanes = slice(h * V_DIM, (h + 1) * V_DIM)
        heads.append(_merge(o[2 * h:2 * h + 1, lanes], o[2 * h + 1:2 * h + 2, lanes], lam, subln_ref[...], lam_init))
    o_ref[0] = jnp.concatenate(heads, axis=1).astype(BF16)


def _attn_sample_finish(q_rows, k_new, v_new, m, l, acc, lams, subln, lam_init):
    b = q_rows.shape[0]
    seq3 = lambda w: pl.BlockSpec((1, 1, w), lambda s: (s, 0, 0))
    sub3 = lambda w: pl.BlockSpec((1, N_SUB, w), lambda s: (s, 0, 0))
    small = pl.BlockSpec((1, HEAD_DIM), lambda s: (0, 0))
    out = pl.pallas_call(
        functools.partial(_attn_sample_finish_body, lam_init=lam_init),
        grid=(b,),
        in_specs=[sub3(D_MODEL), seq3(D_MODEL), seq3(D_MODEL), sub3(LANES), sub3(LANES), sub3(D_MODEL),
                  small, small, small, small, pl.BlockSpec((1, V_DIM), lambda s: (0, 0))],
        out_specs=seq3(D_MODEL),
        out_shape=jax.ShapeDtypeStruct((b, 1, D_MODEL), BF16),
        compiler_params=_cparams("parallel"),
        name="attn_sample_finish",
    )(q_rows, k_new.reshape(b, 1, D_MODEL), v_new.reshape(b, 1, D_MODEL), m, l, acc,
      *[a.astype(F32).reshape(1, HEAD_DIM) for a in lams], subln.astype(F32).reshape(1, V_DIM))
    return out.reshape(b, D_MODEL)


def _out_proj_body(x_ref, a_ref, w_ref, o_ref):
    o_ref[...] = x_ref[...] + _dot(a_ref[...], w_ref[...])


def _out_proj(x, a, w_o):
    rows = x.shape[0]
    tm = _row_tile(rows)
    row_spec = pl.BlockSpec((tm, D_MODEL), lambda i: (i, 0))
    return pl.pallas_call(
        _out_proj_body,
        grid=(rows // tm,),
        in_specs=[row_spec, row_spec, _resident((D_MODEL, D_MODEL))],
        out_specs=row_spec,
        out_shape=jax.ShapeDtypeStruct((rows, D_MODEL), F32),
        compiler_params=_cparams("parallel"),
        name="out_proj",
    )(x, a, w_o)


def _block_diag_rows(q):
    head = jnp.arange(D_MODEL) // HEAD_DIM
    keep = head[None, :] == jnp.arange(N_SUB)[:, None]
    return jnp.where(keep[None], q[:, None, :], jnp.zeros((), q.dtype))


def kernel(x_prompt, x_sample, state_s5_re, state_s5_im, cache_k, cache_v, page_table, ffn_norm, ffn_w_in, ffn_w_out, mix_norm, s5_lambda_re, s5_lambda_im, s5_log_dt, s5_b_re, s5_b_im, s5_c_re, s5_c_im, s5_d, s5_w_glu, s5_b_glu, attn_w_qkv, attn_q_norm, attn_k_norm, attn_lambda_q1, attn_lambda_k1, attn_lambda_q2, attn_lambda_k2, attn_subln, attn_w_o):
    bsz, seq, _ = x_prompt.shape
    dbs, dseq, _ = x_sample.shape
    assert dseq == 1, "the sample group decodes one token per sequence"
    assert SUBLANES % bsz == 0 and seq % S5_CHUNK == 0
    depth = ffn_norm.shape[0]
    past_len = page_table.shape[1] * cache_k.shape[2]

    xp = x_prompt.reshape(bsz * seq, D_MODEL)
    xs = x_sample.reshape(dbs * dseq, D_MODEL)
    w_in = ffn_w_in.astype(BF16)
    w_out = ffn_w_out.astype(BF16)
    f32n = ffn_norm.astype(F32)
    mixn = mix_norm.astype(F32)

    s5_re_p, s5_im_p, s5_re_s, s5_im_s = [], [], [], []
    k_p, v_p, k_s, v_s = [], [], [], []
    for i in range(depth):
        a = i // N_MIXERS
        if i % N_MIXERS == 0:
            xp, hp = _ffn(xp, f32n[i, 0], w_in[i, 0], w_out[i, 0], mixn[i], F32)
            xs, hs = _ffn(xs, f32n[i, 0], w_in[i, 0], w_out[i, 0], mixn[i], F32)
            prm = (s5_lambda_re[a], s5_lambda_im[a], s5_log_dt[a], s5_b_re[a], s5_b_im[a], s5_c_re[a], s5_c_im[a])
            w_glu = s5_w_glu[a].astype(BF16)
            yp, hr_p, hi_p = _s5_prompt(hp, _s5_chunk_tables(*prm), bsz)
            ys, hr_s, hi_s = _s5_sample(hs, state_s5_re[a], state_s5_im[a], *prm)
            xp = _s5_post(xp, yp, mixn[i], s5_d[a], w_glu, s5_b_glu[a])
            xs = _s5_post(xs, ys, mixn[i], s5_d[a], w_glu, s5_b_glu[a])
            s5_re_p.append(hr_p)
            s5_im_p.append(hi_p)
            s5_re_s.append(hr_s)
            s5_im_s.append(hi_s)
        else:
            lam_init = 0.8 - 0.6 * math.exp(-0.3 * i)
            lams = (attn_lambda_q1[a], attn_lambda_k1[a], attn_lambda_q2[a], attn_lambda_k2[a])
            w_qkv = attn_w_qkv[a].astype(BF16)
            w_o = attn_w_o[a].astype(BF16)
            cos_p, sin_p = _rope_tables(jnp.arange(seq))
            cos_s, sin_s = _rope_tables(jnp.tile(past_len + jnp.arange(dseq), dbs))
            xs, hs = _ffn(xs, f32n[i, 0], w_in[i, 0], w_out[i, 0], mixn[i], BF16)
            qs, ksf, ksb, vsf, vsb = _qkv(hs, w_qkv, attn_q_norm[a], attn_k_norm[a], cos_s, sin_s)
            q_rows = _block_diag_rows(qs)
            ck_t = jnp.transpose(cache_k, (0, 1, 3, 4, 2))
            cv_rows = cache_v.reshape(cache_v.shape[:2] + (cache_v.shape[2] * N_HEADS, V_DIM))
            first = dbs // 2
            xp, hp, m0, l0, acc0 = _ffn_pages(xp, f32n[i, 0], w_in[i, 0], w_out[i, 0], q_rows, ck_t, cv_rows, a,
                                              page_table, 0, first, mixn[i], BF16)
            qp, kpf, kpb, vpf, vpb = _qkv(hp, w_qkv, attn_q_norm[a], attn_k_norm[a], cos_p, sin_p, seq_minor_k=True)
            op = _attn_prompt(qp, kpb, vpb, lams, attn_subln[a], lam_init, bsz)
            xp = _out_proj(xp, op, w_o)
            xp, m1, l1, acc1 = _ffn_pages(xp, f32n[i, 1], w_in[i, 1], w_out[i, 1], q_rows, ck_t, cv_rows, a,
                                          page_table, first, dbs - first)
            m, l, acc = (jnp.concatenate(parts, axis=0) for parts in ((m0, m1), (l0, l1), (acc0, acc1)))
            os_ = _attn_sample_finish(q_rows, ksb, vsb, m, l, acc, lams, attn_subln[a], lam_init)
            xs = _out_proj(xs, os_, w_o)
            xs = _ffn(xs, f32n[i, 1], w_in[i, 1], w_out[i, 1])
            k_p.append(kpf.transpose(0, 3, 1, 2))
            v_p.append(vpf.reshape(bsz, seq, N_HEADS, V_DIM))
            k_s.append(ksf.reshape(dbs, dseq, N_SUB, HEAD_DIM))
            v_s.append(vsf.reshape(dbs, dseq, N_HEADS, V_DIM))
            continue
        xp = _ffn(xp, f32n[i, 1], w_in[i, 1], w_out[i, 1])
        xs = _ffn(xs, f32n[i, 1], w_in[i, 1], w_out[i, 1])
    return (xp.reshape(bsz, seq, D_MODEL), xs.reshape(dbs, dseq, D_MODEL),
            jnp.stack(s5_re_p), jnp.stack(s5_im_p), jnp.stack(s5_re_s), jnp.stack(s5_im_s),
            jnp.stack(k_p), jnp.stack(v_p), jnp.stack(k_s), jnp.stack(v_s))
```

```python
import functools
import math

import jax
import jax.numpy as jnp
from jax import lax
from jax.experimental import pallas as pl
from jax.experimental.pallas import tpu as pltpu

F32 = jnp.float32
BF16 = jnp.bfloat16

D_MODEL = 1024
D_FF = 2816
S5_GROUP = 16
S5_GROUPS = D_MODEL // S5_GROUP
S5_STATE = 64
S5_CHUNK = 16
N_HEADS = 8
N_SUB = 2 * N_HEADS
HEAD_DIM = D_MODEL // N_SUB
V_DIM = 2 * HEAD_DIM
ATTN_SCALE = HEAD_DIM ** -0.5
ROPE_THETA = 10000.0
EPS = 1e-6
NEG_INF = -1e30
N_MIXERS = 2

LANES = 128
SUBLANES = 8
MXU_TILE = 256
FF_CHUNK = MXU_TILE
ROW_TILE = 512
ATTN_Q_TILE = 1024
ATTN_K_TILE = 1024
HOSTED_GROUP_PAGES = 4
HOSTED_RING = 4
LANE_GROUPS = LANES // S5_GROUP
S5_SEQS_PER_STEP = 2
S5_RELAYOUT_ROWS = 64
VMEM_LIMIT = 48 * 1024 * 1024
VMEM_LIMIT_HOSTED = 56 * 1024 * 1024


def _cparams(*sem, vmem=None):
    return pltpu.CompilerParams(dimension_semantics=sem, vmem_limit_bytes=vmem or VMEM_LIMIT)


def _resident(shape):
    zeros = (0,) * len(shape)
    return pl.BlockSpec(shape, lambda *_: zeros, pipeline_mode=pl.Buffered(1))


def _rms(x, g):
    return x * lax.rsqrt(jnp.mean(x * x, axis=-1, keepdims=True) + EPS) * g


def _dot(a, b):
    return jnp.dot(a, b, preferred_element_type=F32)


def _dot_nt(a, b):
    return lax.dot_general(a, b, (((1,), (1,)), ((), ())), preferred_element_type=F32)


def _row_tile(rows):
    return ROW_TILE if rows % ROW_TILE == 0 else rows


def _ffn_body(x_ref, g_ref, win_ref, wout_ref, *rest, norm_dtype):
    if norm_dtype is None:
        o_ref, act_ref = rest
    else:
        gm_ref, o_ref, hn_ref, act_ref = rest
    x = x_ref[...]
    xn = _rms(x, g_ref[...]).astype(BF16)
    for c in range(D_FF // FF_CHUNK):
        lo = c * FF_CHUNK
        g = _dot(xn, win_ref[:, lo:lo + FF_CHUNK])
        u = _dot(xn, win_ref[:, D_FF + lo:D_FF + lo + FF_CHUNK])
        act_ref[:, lo:lo + FF_CHUNK] = (g * jax.nn.sigmoid(g) * u).astype(BF16)
    xo = x + 0.5 * _dot(act_ref[...], wout_ref[...])
    o_ref[...] = xo
    if norm_dtype is not None:
        hn_ref[...] = _rms(xo, gm_ref[...]).astype(norm_dtype)


def _ffn(x, g, w_in, w_out, g_mix=None, norm_dtype=None):
    rows = x.shape[0]
    tm = _row_tile(rows)
    row_spec = pl.BlockSpec((tm, D_MODEL), lambda i: (i, 0))
    in_specs = [row_spec, _resident((1, D_MODEL)), _resident((D_MODEL, 2 * D_FF)), _resident((D_FF, D_MODEL))]
    args = [x, g.reshape(1, D_MODEL), w_in, w_out]
    out_shape = [jax.ShapeDtypeStruct((rows, D_MODEL), F32)]
    out_specs = [row_spec]
    if norm_dtype is not None:
        in_specs.append(_resident((1, D_MODEL)))
        args.append(g_mix.reshape(1, D_MODEL))
        out_shape.append(jax.ShapeDtypeStruct((rows, D_MODEL), norm_dtype))
        out_specs.append(row_spec)
    res = pl.pallas_call(
        functools.partial(_ffn_body, norm_dtype=norm_dtype),
        grid=(rows // tm,),
        in_specs=in_specs,
        out_specs=out_specs,
        out_shape=out_shape,
        scratch_shapes=[pltpu.VMEM((tm, D_FF), BF16)],
        compiler_params=_cparams("parallel"),
        name="ffn_norm" if norm_dtype is not None else "ffn",
    )(*args)
    return res if norm_dtype is not None else res[0]


def _s5_discretise(lam_re, lam_im, log_dt, b_re, b_im):
    dt = jnp.exp(log_dt.astype(F32))[:, None]
    lr = lam_re.astype(F32)
    li = lam_im.astype(F32)
    mag = jnp.exp(lr * dt)
    a_re = mag * jnp.cos(li * dt)
    a_im = mag * jnp.sin(li * dt)
    den = lr * lr + li * li
    nr = a_re - 1.0
    z_re = (nr * lr + a_im * li) / den
    z_im = (a_im * lr - nr * li) / den
    br = b_re.astype(F32)
    bi = b_im.astype(F32)
    bb_re = z_re[..., None] * br - z_im[..., None] * bi
    bb_im = z_re[..., None] * bi + z_im[..., None] * br
    return a_re, a_im, bb_re, bb_im, lr * dt, li * dt


def _s5_chunk_tables(lam_re, lam_im, log_dt, b_re, b_im, c_re, c_im):
    L, G, N, C = S5_CHUNK, S5_GROUPS, S5_STATE, S5_GROUP
    _, _, bb_re, bb_im, ldr, ldi = _s5_discretise(lam_re, lam_im, log_dt, b_re, b_im)
    tau = jnp.arange(L + 1, dtype=F32)[:, None, None]
    mag = jnp.exp(ldr[None] * tau)
    p_re = mag * jnp.cos(ldi[None] * tau)
    p_im = mag * jnp.sin(ldi[None] * tau)
    cr = c_re.astype(F32)
    ci = c_im.astype(F32)
    hi = lax.Precision.HIGHEST
    ca_re = cr[None] * p_re[:, :, None, :] - ci[None] * p_im[:, :, None, :]
    ca_im = cr[None] * p_im[:, :, None, :] + ci[None] * p_re[:, :, None, :]
    ca_rows = jnp.concatenate([ca_re[:L], ca_im[:L]], axis=-1).transpose(1, 3, 0, 2).reshape(G, 2 * N, L * C)
    bb_rows = jnp.concatenate([bb_re, -bb_im], axis=1)
    resp = jnp.einsum('gki,gkx->gix', bb_rows, ca_rows, precision=hi)
    toep = jnp.stack([jnp.pad(resp[:, :, :(L - s) * C], ((0, 0), (0, 0), (s * C, 0))) for s in range(L)], axis=1)
    toep = toep.reshape(G, L * C, L * C)
    pw_re = p_re[L - 1 - jnp.arange(L)]
    pw_im = p_im[L - 1 - jnp.arange(L)]
    w_re = pw_re[..., None] * bb_re[None] - pw_im[..., None] * bb_im[None]
    w_im = pw_re[..., None] * bb_im[None] + pw_im[..., None] * bb_re[None]
    w_re = w_re.transpose(1, 0, 3, 2).reshape(G, L * C, N)
    w_im = w_im.transpose(1, 0, 3, 2).reshape(G, L * C, N)
    w_in = jnp.concatenate([w_re, w_im, w_im, w_re], axis=-1)
    v_re = ca_re[1:].transpose(1, 3, 0, 2).reshape(G, N, L * C)
    v_im = -ca_im[1:].transpose(1, 3, 0, 2).reshape(G, N, L * C)
    v_out = jnp.concatenate([v_re, v_im], axis=1)
    al_re, al_im = p_re[L], p_im[L]
    decay = jnp.stack([jnp.concatenate([al_re, al_re], -1),
                       jnp.concatenate([-al_im, al_im], -1),
                       jnp.concatenate([al_im, -al_im], -1)], axis=1)
    return toep.astype(BF16), w_in.astype(BF16), v_out.astype(BF16), decay


def _block_transpose(xs):
    n = len(xs)
    block = lax.broadcasted_iota(jnp.int32, xs[0].shape, 1) // S5_GROUP
    d = n // 2
    while d >= 1:
        upper = (block // d) % 2 == 1
        nxt = list(xs)
        for i in range(n):
            if i & d == 0:
                a, b = xs[i], xs[i | d]
                nxt[i] = jnp.where(upper, pltpu.roll(b, d * S5_GROUP, 1), a)
                nxt[i | d] = jnp.where(upper, b, pltpu.roll(a, LANES - d * S5_GROUP, 1))
        xs = nxt
        d //= 2
    return xs


def _s5_prompt_body(hn_ref, toep_ref, win_ref, vout_ref, dec_ref, y_ref, hfin_ref, u_scr, sy_scr, hp_scr, *, sps):
    lg = LANE_GROUPS
    rows = u_scr.shape[2]
    nch = rows // sps
    n2 = 2 * S5_STATE
    half = S5_CHUNK // 2
    rb = S5_RELAYOUT_ROWS
    blocks_per_seq = nch // rb

    def chunk_rows(blk):
        seq = blk // blocks_per_seq
        c0 = (blk % blocks_per_seq) * rb
        return c0 * sps + seq, (seq * nch + c0) * S5_CHUNK

    def gather(blk, _):
        r0, t0 = chunk_rows(blk)
        for h in range(2):
            xs = [hn_ref[pl.ds(t0 + h * half + tt, rb, stride=S5_CHUNK), :] for tt in range(half)]
            for g, y in enumerate(_block_transpose(xs)):
                u_scr[g, h, pl.ds(r0, rb, stride=sps), :] = y
        return _

    lax.fori_loop(0, sps * blocks_per_seq, gather, 0)

    def chunk_inputs(g):
        return jnp.concatenate([u_scr[g, 0], u_scr[g, 1]], axis=1).astype(BF16)

    for g in range(lg):
        s = _dot(chunk_inputs(g), win_ref[g])
        sy_scr[g, 0] = s[:, :n2]
        sy_scr[g, 1] = s[:, n2:]
    decs = [dec_ref[g] for g in range(lg)]
    cpv = SUBLANES // sps

    def step(i, carry):
        r0 = pl.multiple_of(i * SUBLANES, SUBLANES)
        new = []
        for g in range(lg):
            h, hs = carry[g]
            a1 = decs[g][0:1, :]
            a2 = decs[g][1:2, :]
            a2s = decs[g][2:3, :]
            slab = sy_scr[g, 0, pl.ds(r0, SUBLANES), :]
            slab_s = sy_scr[g, 1, pl.ds(r0, SUBLANES), :]
            prev = []
            for c in range(cpv):
                prev.append(h)
                rs = slice(c * sps, (c + 1) * sps)
                h, hs = (a1 * h + a2 * hs + slab[rs], a1 * hs + a2s * h + slab_s[rs])
            hp_scr[g, pl.ds(r0, SUBLANES), :] = jnp.concatenate(prev, axis=0)
            new.append((h, hs))
        return tuple(new)

    zero = jnp.zeros((sps, n2), F32)
    fin = lax.fori_loop(0, rows // SUBLANES, step, tuple((zero, zero) for _ in range(lg)))
    for g in range(lg):
        hfin_ref[0, g] = fin[g][0]
        y = _dot(chunk_inputs(g), toep_ref[g]) + _dot(hp_scr[g].astype(BF16), vout_ref[g])
        sy_scr[g, 0] = y[:, :n2]
        sy_scr[g, 1] = y[:, n2:]

    def scatter(blk, _):
        r0, t0 = chunk_rows(blk)
        for h in range(2):
            ys = [sy_scr[g, h, pl.ds(r0, rb, stride=sps), :] for g in range(lg)]
            for tt, x in enumerate(_block_transpose(ys)):
                y_ref[pl.ds(t0 + h * half + tt, rb, stride=S5_CHUNK), :] = x
        return _

    lax.fori_loop(0, sps * blocks_per_seq, scatter, 0)


def _s5_prompt(hn, tables, bsz):
    toep, w_in, v_out, decay = tables
    G, L, N = S5_GROUPS, S5_CHUNK, S5_STATE
    t = hn.shape[0] // bsz
    sps = S5_SEQS_PER_STEP
    assert bsz % sps == 0 and SUBLANES % sps == 0 and (t // L) % S5_RELAYOUT_ROWS == 0
    rows = (t // L) * sps
    lg = LANE_GROUPS
    lc = L * S5_GROUP
    tok_spec = pl.BlockSpec((sps * t, LANES), lambda j, s: (s, j))
    gspec = lambda *shape: pl.BlockSpec((lg,) + shape, lambda j, s: (j,) + (0,) * len(shape))
    y, hfin = pl.pallas_call(
        functools.partial(_s5_prompt_body, sps=sps),
        grid=(G // lg, bsz // sps),
        in_specs=[tok_spec, gspec(lc, lc), gspec(lc, lc), gspec(2 * N, lc), gspec(3, 2 * N)],
        out_specs=[tok_spec, pl.BlockSpec((1, lg, sps, 2 * N), lambda j, s: (s, j, 0, 0))],
        out_shape=[jax.ShapeDtypeStruct(hn.shape, F32), jax.ShapeDtypeStruct((bsz // sps, G, sps, 2 * N), F32)],
        scratch_shapes=[pltpu.VMEM((lg, 2, rows, LANES), F32), pltpu.VMEM((lg, 2, rows, LANES), F32),
                        pltpu.VMEM((lg, rows, 2 * N), F32)],
        compiler_params=_cparams("parallel", "parallel"),
        name="s5_prompt",
    )(hn, toep, w_in, v_out, decay)
    hfin = hfin.transpose(0, 2, 1, 3).reshape(bsz, G, 2 * N)
    return y, hfin[:, :, :N], hfin[:, :, N:]


def _s5_sample_body(u_ref, bbr_ref, bbi_ref, ar_ref, ai_ref, h0r_ref, h0i_ref, cr_ref, ci_ref,
                    y_ref, hr_ref, hi_ref):
    hi = lax.Precision.HIGHEST
    u = u_ref[...]
    bu_re = jnp.einsum('gbc,gnc->gbn', u, bbr_ref[...], precision=hi, preferred_element_type=F32)
    bu_im = jnp.einsum('gbc,gnc->gbn', u, bbi_ref[...], precision=hi, preferred_element_type=F32)
    ar, ai = ar_ref[...], ai_ref[...]
    h0r, h0i = h0r_ref[...], h0i_ref[...]
    hr = bu_re + ar * h0r - ai * h0i
    hi_ = bu_im + ar * h0i + ai * h0r
    hr_ref[...] = hr
    hi_ref[...] = hi_
    y_ref[...] = (jnp.einsum('gbn,gcn->gbc', hr, cr_ref[...], precision=hi, preferred_element_type=F32)
                  - jnp.einsum('gbn,gcn->gbc', hi_, ci_ref[...], precision=hi, preferred_element_type=F32))


def _s5_sample(hs, h0_re, h0_im, lam_re, lam_im, log_dt, b_re, b_im, c_re, c_im):
    G, C, N = S5_GROUPS, S5_GROUP, S5_STATE
    b = hs.shape[0]
    a_re, a_im, bb_re, bb_im, _, _ = _s5_discretise(lam_re, lam_im, log_dt, b_re, b_im)
    u = hs.reshape(b, G, C).transpose(1, 0, 2)
    args = [u, bb_re, bb_im, a_re[:, None, :], a_im[:, None, :],
            h0_re.astype(F32).transpose(1, 0, 2), h0_im.astype(F32).transpose(1, 0, 2),
            c_re.astype(F32), c_im.astype(F32)]
    full = lambda a: pl.BlockSpec(a.shape, lambda i: (0,) * a.ndim)
    outs = [jax.ShapeDtypeStruct((G, b, C), F32), jax.ShapeDtypeStruct((G, b, N), F32),
            jax.ShapeDtypeStruct((G, b, N), F32)]
    y, hr, hi = pl.pallas_call(
        _s5_sample_body,
        grid=(1,),
        in_specs=[full(a) for a in args],
        out_specs=[full(o) for o in outs],
        out_shape=outs,
        compiler_params=_cparams("arbitrary"),
        name="s5_sample",
    )(*args)
    return y.transpose(1, 0, 2).reshape(b, D_MODEL), hr.transpose(1, 0, 2), hi.transpose(1, 0, 2)


def _s5_post_body(x_ref, y_ref, gm_ref, d_ref, w_ref, b_ref, o_ref):
    x = x_ref[...]
    hp = _rms(x, gm_ref[...])
    z = jax.nn.gelu(y_ref[...] + d_ref[...] * hp)
    zz = _dot(z.astype(BF16), w_ref[...]) + b_ref[...]
    o_ref[...] = x + zz[:, :D_MODEL] * jax.nn.sigmoid(zz[:, D_MODEL:])


def _s5_post(x, y, g_mix, d, w_glu, b_glu):
    rows = x.shape[0]
    tm = _row_tile(rows)
    row_spec = pl.BlockSpec((tm, D_MODEL), lambda i: (i, 0))
    return pl.pallas_call(
        _s5_post_body,
        grid=(rows // tm,),
        in_specs=[row_spec, row_spec, _resident((1, D_MODEL)), _resident((1, D_MODEL)),
                  _resident((D_MODEL, 2 * D_MODEL)), _resident((1, 2 * D_MODEL))],
        out_specs=row_spec,
        out_shape=jax.ShapeDtypeStruct((rows, D_MODEL), F32),
        compiler_params=_cparams("parallel"),
        name="s5_post",
    )(x, y, g_mix.reshape(1, D_MODEL), d.astype(F32).reshape(1, D_MODEL), w_glu, b_glu.astype(F32).reshape(1, -1))


def _rope_tables(pos):
    half = HEAD_DIM // 2
    inv = ROPE_THETA ** (-jnp.arange(half, dtype=F32) / half)
    ang = pos.astype(F32)[:, None] * inv[None, :]
    cos = jnp.cos(ang)
    sin = jnp.sin(ang)
    cos_t = jnp.concatenate([cos, cos, cos, cos], axis=-1)
    sin_t = jnp.concatenate([-sin, sin, -sin, sin], axis=-1)
    return cos_t, sin_t


def _head_sum_matrix():
    head = jnp.arange(D_MODEL) // HEAD_DIM
    return (head[:, None] == head[None, :]).astype(BF16)


def _qkv_body(hn_ref, w_ref, qg_ref, kg_ref, hsum_ref, cos_ref, sin_ref,
              q_ref, kf_ref, kb_ref, vf_ref, vb_ref, *, seq_minor_k):
    hn = hn_ref[...]
    cos = cos_ref[...]
    sin = sin_ref[...]
    lane = lax.broadcasted_iota(jnp.int32, cos.shape, 1)
    low_half = (lane % HEAD_DIM) < (HEAD_DIM // 2)

    def norm_rope(x, g):
        ss = _dot((x * x).astype(BF16), hsum_ref[...])
        y = x * lax.rsqrt(ss * (1.0 / HEAD_DIM) + EPS) * g
        outs = []
        for j in range(D_MODEL // LANES):
            t = y[:, j * LANES:(j + 1) * LANES]
            below = pltpu.roll(t, HEAD_DIM // 2, 1)
            above = pltpu.roll(t, LANES - HEAD_DIM // 2, 1)
            outs.append(t * cos + jnp.where(low_half, above, below) * sin)
        return jnp.concatenate(outs, axis=1)

    q = norm_rope(_dot(hn, w_ref[:, 0:D_MODEL]), qg_ref[...])
    q_ref[...] = (q * ATTN_SCALE).astype(BF16)
    k = norm_rope(_dot(hn, w_ref[:, D_MODEL:2 * D_MODEL]), kg_ref[...])
    if seq_minor_k:
        kf_ref[0] = k.T.reshape(kf_ref.shape[1:])
    else:
        kf_ref[...] = k
    kb_ref[...] = k.astype(BF16)
    v = _dot(hn, w_ref[:, 2 * D_MODEL:3 * D_MODEL])
    vf_ref[...] = v
    vb_ref[...] = v.astype(BF16)


def _qkv(hn, w_qkv, q_norm, k_norm, cos_t, sin_t, seq_minor_k=False):
    rows = hn.shape[0]
    tm = _row_tile(rows)
    tab_blocks = cos_t.shape[0] // tm
    row_spec = pl.BlockSpec((tm, D_MODEL), lambda i: (i, 0))
    tab_spec = pl.BlockSpec((tm, LANES), lambda i: (i % tab_blocks, 0))
    gain = lambda g: jnp.tile(g.astype(F32), N_SUB).reshape(1, D_MODEL)
    outs = [jax.ShapeDtypeStruct((rows, D_MODEL), dt) for dt in (BF16, F32, BF16, F32, BF16)]
    out_specs = [row_spec] * 5
    if seq_minor_k:
        period = cos_t.shape[0]
        outs[1] = jax.ShapeDtypeStruct((rows // period, N_SUB, HEAD_DIM, period), F32)
        out_specs[1] = pl.BlockSpec((1, N_SUB, HEAD_DIM, tm), lambda i: (i // tab_blocks, 0, 0, i % tab_blocks))
    return pl.pallas_call(
        functools.partial(_qkv_body, seq_minor_k=seq_minor_k),
        grid=(rows // tm,),
        in_specs=[row_spec, _resident((D_MODEL, 3 * D_MODEL)), _resident((1, D_MODEL)), _resident((1, D_MODEL)),
                  _resident((D_MODEL, D_MODEL)), tab_spec, tab_spec],
        out_specs=out_specs,
        out_shape=outs,
        compiler_params=_cparams("parallel"),
        name="qkv",
    )(hn, w_qkv, gain(q_norm), gain(k_norm), _head_sum_matrix(), cos_t, sin_t)


def _lambda_full(lq1_ref, lk1_ref, lq2_ref, lk2_ref, lam_init):
    s1 = jnp.sum(lq1_ref[...] * lk1_ref[...], axis=-1, keepdims=True)
    s2 = jnp.sum(lq2_ref[...] * lk2_ref[...], axis=-1, keepdims=True)
    return jnp.exp(s1) - jnp.exp(s2) + lam_init


def _merge(o1, o2, lam, subln, lam_init):
    out = o1 - lam * o2
    return _rms(out, subln) * (1.0 - lam_init)


def _attn_prompt_body(q_ref, k_ref, v_ref, lq1_ref, lk1_ref, lq2_ref, lk2_ref, subln_ref, o_ref, *, lam_init, tk, nq):
    tq = q_ref.shape[0]
    qi = pl.program_id(2)
    kt_per_q = tq // tk
    qt = q_ref[...].astype(F32).T.astype(BF16)
    dim = lax.broadcasted_iota(jnp.int32, qt.shape, 0)
    zero = jnp.zeros_like(qt)
    qst = jnp.concatenate([jnp.where(dim < HEAD_DIM, qt, zero), jnp.where(dim >= HEAD_DIM, qt, zero)], axis=1)

    span = MXU_TILE
    n_span = 2 * tq // span

    def visible_keys(c, diagonal):
        if diagonal is None:
            return tk
        return max(0, min(tk, (c * span) % tq + span - diagonal * tk))

    def scores(j, diagonal):
        out = []
        for c in range(n_span):
            nk = visible_keys(c, diagonal)
            out.append(_dot(k_ref[pl.ds(j * tk, nk), :], qst[:, c * span:(c + 1) * span]) if nk else None)
        return out

    def absorb(j, tile_scores, carry, diagonal):
        new = []
        for c, (s, (m, l, acc)) in enumerate(zip(tile_scores, carry)):
            if s is None:
                new.append((m, l, acc))
                continue
            if diagonal is not None:
                key = lax.broadcasted_iota(jnp.int32, s.shape, 0) + diagonal * tk
                qry = lax.broadcasted_iota(jnp.int32, s.shape, 1) + (c * span) % tq
                s = jnp.where(key <= qry, s, NEG_INF)
            m_new = jnp.maximum(m, jnp.max(s, axis=0, keepdims=True))
            alpha = jnp.exp(m - m_new)
            p = jnp.exp(s - m_new)
            l_new = l * alpha + jnp.sum(p, axis=0, keepdims=True)
            vj = v_ref[pl.ds(j * tk, s.shape[0]), :]
            pv = lax.dot_general(vj, p.astype(BF16), (((0,), (0,)), ((), ())), preferred_element_type=F32)
            new.append((m_new, l_new, acc * alpha + pv))
        return new

    def run(n_tiles):
        carry = [(jnp.full((1, span), NEG_INF, F32), jnp.zeros((1, span), F32), jnp.zeros((V_DIM, span), F32))
                 for _ in range(n_span)]
        def diagonal(j):
            d = j - (n_tiles - kt_per_q)
            return d if d >= 0 else None

        tile_scores = scores(0, diagonal(0))
        for j in range(n_tiles):
            next_scores = scores(j + 1, diagonal(j + 1)) if j + 1 < n_tiles else None
            carry = absorb(j, tile_scores, carry, diagonal(j))
            tile_scores = next_scores
        o = jnp.concatenate([acc / l for _, l, acc in carry], axis=1)
        lam = _lambda_full(lq1_ref, lk1_ref, lq2_ref, lk2_ref, lam_init)
        out = o[:, :tq] - lam * o[:, tq:]
        out = out * lax.rsqrt(jnp.mean(out * out, axis=0, keepdims=True) + EPS) * subln_ref[...] * (1.0 - lam_init)
        o_ref[...] = out.T.astype(BF16)

    for n in range(nq):
        pl.when(qi == n)(functools.partial(run, (n + 1) * kt_per_q))


def _attn_prompt(q, k, v, lams, subln, lam_init, bsz):
    rows = q.shape[0]
    t = rows // bsz
    tq = ATTN_Q_TILE if t % ATTN_Q_TILE == 0 else t
    tk = ATTN_K_TILE if tq % ATTN_K_TILE == 0 else tq
    nq = t // tq
    small = pl.BlockSpec((1, HEAD_DIM), lambda b, h, i: (0, 0))
    return pl.pallas_call(
        functools.partial(_attn_prompt_body, lam_init=lam_init, tk=tk, nq=nq),
        grid=(bsz, N_HEADS, nq),
        in_specs=[pl.BlockSpec((tq, V_DIM), lambda b, h, i: (b * nq + i, h)),
                  pl.BlockSpec((t, V_DIM), lambda b, h, i: (b, h)),
                  pl.BlockSpec((t, V_DIM), lambda b, h, i: (b, h)),
                  small, small, small, small,
                  pl.BlockSpec((V_DIM, 1), lambda b, h, i: (0, 0))],
        out_specs=pl.BlockSpec((tq, V_DIM), lambda b, h, i: (b * nq + i, h)),
        out_shape=jax.ShapeDtypeStruct((rows, D_MODEL), BF16),
        compiler_params=_cparams("parallel", "parallel", "arbitrary"),
        name="attn_prompt",
    )(q, k, v, *[a.astype(F32).reshape(1, HEAD_DIM) for a in lams], subln.astype(F32).reshape(V_DIM, 1))


def _pages_scores(qr, keys_t, n):
    return _dot(qr, jnp.concatenate([keys_t(i) for i in range(n)], axis=1))


def _pages_absorb(s, values, n, m_ref, l_ref, acc_ref):
    m_old = m_ref[0][:, 0:1]
    l_old = l_ref[0][:, 0:1]
    m_new = jnp.maximum(m_old, jnp.max(s, axis=-1, keepdims=True))
    alpha = jnp.exp(m_old - m_new)
    p = jnp.exp(s - m_new)
    l_new = l_old * alpha + jnp.sum(p, axis=-1, keepdims=True)
    pb = p.astype(BF16)
    pv = _dot(pb, jnp.concatenate([values(i) for i in range(n)], axis=0))
    acc_ref[0] = acc_ref[0] * alpha + pv
    m_ref[0] = jnp.broadcast_to(m_new, m_ref.shape[1:])
    l_ref[0] = jnp.broadcast_to(l_new, l_ref.shape[1:])


def _ffn_pages_body(pt_ref, x_ref, g_ref, win_ref, wout_ref, *rest, norm_dtype, layer, seq0, steps_per_seq, groups):
    if norm_dtype is None:
        qr_ref, ck_ref, cv_ref, o_ref, m_ref, l_ref, acc_ref, act_ref, kbuf, vbuf, sem = rest
    else:
        gm_ref, qr_ref, ck_ref, cv_ref, o_ref, hn_ref, m_ref, l_ref, acc_ref, act_ref, kbuf, vbuf, sem = rest
    i = pl.program_id(0)
    last = pl.num_programs(0) - 1
    gp = HOSTED_GROUP_PAGES
    page = kbuf.shape[-1]

    def copies(step, grp, slot):
        seq = seq0 + step // steps_per_seq
        first = (step % steps_per_seq) * (groups * gp) + grp * gp
        out = []
        for pg in range(gp):
            phys = pt_ref[seq, first + pg]
            out.append(pltpu.make_async_copy(ck_ref.at[layer, phys], kbuf.at[slot, pg], sem.at[slot, 0]))
            out.append(pltpu.make_async_copy(cv_ref.at[layer, phys], vbuf.at[slot, pg], sem.at[slot, 1]))
        return out

    def start(step, grp, slot):
        for c in copies(step, grp, slot):
            c.start()

    ring = kbuf.shape[0]
    ahead = ring - 1

    @pl.when(i == 0)
    def _():
        for grp in range(ahead):
            start(i, grp, grp)

    @pl.when(i % steps_per_seq == 0)
    def _():
        m_ref[...] = jnp.full(m_ref.shape, NEG_INF, F32)
        l_ref[...] = jnp.zeros(l_ref.shape, F32)
        acc_ref[...] = jnp.zeros(acc_ref.shape, F32)

    x = x_ref[...]
    xn = _rms(x, g_ref[...]).astype(BF16)
    qr = qr_ref[0]
    n_chunks = D_FF // FF_CHUNK
    def wait(grp):
        for c in copies(i, grp, grp % ring):
            c.wait()

    def scores(grp):
        slot = grp % ring
        return _pages_scores(qr, lambda pg: kbuf[slot, pg].reshape(D_MODEL, page).astype(BF16), gp)

    def values(grp):
        slot = grp % ring

        def page_values(pg):
            heads = [vbuf[slot, pg, pl.ds(h, page, stride=N_HEADS), :] for h in range(N_HEADS)]
            return jnp.concatenate(heads, axis=1).astype(BF16)
        return page_values

    wait(0)
    s = scores(0)
    for grp in range(groups):
        nxt = grp + ahead
        if nxt < groups:
            start(i, nxt, nxt % ring)
        else:
            @pl.when(i < last)
            def _(nxt=nxt):
                start(i + 1, nxt - groups, nxt % ring)
        if grp + 1 < groups:
            wait(grp + 1)
        for c in range(grp * n_chunks // groups, (grp + 1) * n_chunks // groups):
            lo = c * FF_CHUNK
            g = _dot(xn, win_ref[:, lo:lo + FF_CHUNK])
            u = _dot(xn, win_ref[:, D_FF + lo:D_FF + lo + FF_CHUNK])
            act_ref[:, lo:lo + FF_CHUNK] = (g * jax.nn.sigmoid(g) * u).astype(BF16)
        s_next = scores(grp + 1) if grp + 1 < groups else None
        _pages_absorb(s, values(grp), gp, m_ref, l_ref, acc_ref)
        s = s_next
    xo = x + 0.5 * _dot(act_ref[...], wout_ref[...])
    o_ref[...] = xo
    if norm_dtype is not None:
        hn_ref[...] = _rms(xo, gm_ref[...]).astype(norm_dtype)


def _ffn_pages(x, g, w_in, w_out, q_rows, cache_k_t, cache_v_rows, layer, page_table, seq0, n_seq,
               g_mix=None, norm_dtype=None):
    rows = x.shape[0]
    tm = ROW_TILE
    steps = rows // tm
    n_pages = page_table.shape[1]
    page = cache_k_t.shape[-1]
    assert rows % tm == 0 and steps % n_seq == 0
    steps_per_seq = steps // n_seq
    groups = n_pages // steps_per_seq // HOSTED_GROUP_PAGES
    assert groups * HOSTED_GROUP_PAGES * steps_per_seq == n_pages and groups % HOSTED_RING == 0
    row_spec = pl.BlockSpec((tm, D_MODEL), lambda i, pt: (i, 0))
    res = lambda shape: pl.BlockSpec(shape, lambda i, pt: (0,) * len(shape), pipeline_mode=pl.Buffered(1))
    seq_spec = lambda w, off: pl.BlockSpec((1, N_SUB, w), lambda i, pt: (off + i // steps_per_seq, 0, 0))
    hbm = pl.BlockSpec(memory_space=pl.ANY)
    in_specs = [row_spec, res((1, D_MODEL)), res((D_MODEL, 2 * D_FF)), res((D_FF, D_MODEL))]
    args = [x, g.reshape(1, D_MODEL), w_in, w_out]
    out_shape = [jax.ShapeDtypeStruct((rows, D_MODEL), F32)]
    out_specs = [row_spec]
    if norm_dtype is not None:
        in_specs.append(res((1, D_MODEL)))
        args.append(g_mix.reshape(1, D_MODEL))
        out_shape.append(jax.ShapeDtypeStruct((rows, D_MODEL), norm_dtype))
        out_specs.append(row_spec)
    in_specs += [seq_spec(D_MODEL, seq0), hbm, hbm]
    args += [q_rows, cache_k_t, cache_v_rows]
    out_shape += [jax.ShapeDtypeStruct((n_seq, N_SUB, LANES), F32), jax.ShapeDtypeStruct((n_seq, N_SUB, LANES), F32),
                  jax.ShapeDtypeStruct((n_seq, N_SUB, D_MODEL), F32)]
    out_specs += [seq_spec(LANES, 0), seq_spec(LANES, 0), seq_spec(D_MODEL, 0)]
    gpp = HOSTED_GROUP_PAGES
    grid_spec = pltpu.PrefetchScalarGridSpec(
        num_scalar_prefetch=1,
        grid=(steps,),
        in_specs=in_specs,
        out_specs=out_specs,
        scratch_shapes=[pltpu.VMEM((tm, D_FF), BF16),
                        pltpu.VMEM((HOSTED_RING, gpp, N_SUB, HEAD_DIM, page), F32),
                        pltpu.VMEM((HOSTED_RING, gpp, page * N_HEADS, V_DIM), F32),
                        pltpu.SemaphoreType.DMA((HOSTED_RING, 2))],
    )
    return pl.pallas_call(
        functools.partial(_ffn_pages_body, norm_dtype=norm_dtype, layer=layer, seq0=seq0,
                          steps_per_seq=steps_per_seq, groups=groups),
        grid_spec=grid_spec,
        out_shape=out_shape,
        compiler_params=_cparams("arbitrary", vmem=VMEM_LIMIT_HOSTED),
        name="ffn_pages_norm" if norm_dtype is not None else "ffn_pages",
    )(page_table, *args)


def _attn_sample_finish_body(qr_ref, kn_ref, vn_ref, m_ref, l_ref, acc_ref,
                             lq1_ref, lk1_ref, lq2_ref, lk2_ref, subln_ref, o_ref, *, lam_init):
    qr = qr_ref[0].astype(F32)
    kn = kn_ref[0].astype(F32)
    vn = vn_ref[0].astype(F32)
    s = jnp.sum(qr * kn, axis=-1, keepdims=True)
    m_old = m_ref[0][:, 0:1]
    l_old = l_ref[0][:, 0:1]
    m_new = jnp.maximum(m_old, s)
    alpha = jnp.exp(m_old - m_new)
    p = jnp.exp(s - m_new)
    l_new = l_old * alpha + p
    acc = acc_ref[0] * alpha + p.astype(BF16).astype(F32) * vn
    o = acc / l_new
    lam = _lambda_full(lq1_ref, lk1_ref, lq2_ref, lk2_ref, lam_init)
    heads = []
    for h in range(N_HEADS):
        lanes = slice(h * V_DIM, (h + 1) * V_DIM)
        heads.append(_merge(o[2 * h:2 * h + 1, lanes], o[2 * h + 1:2 * h + 2, lanes], lam, subln_ref[...], lam_init))
    o_ref[0] = jnp.concatenate(heads, axis=1).astype(BF16)


def _attn_sample_finish(q_rows, k_new, v_new, m, l, acc, lams, subln, lam_init):
    b = q_rows.shape[0]
    seq3 = lambda w: pl.BlockSpec((1, 1, w), lambda s: (s, 0, 0))
    sub3 = lambda w: pl.BlockSpec((1, N_SUB, w), lambda s: (s, 0, 0))
    small = pl.BlockSpec((1, HEAD_DIM), lambda s: (0, 0))
    out = pl.pallas_call(
        functools.partial(_attn_sample_finish_body, lam_init=lam_init),
        grid=(b,),
        in_specs=[sub3(D_MODEL), seq3(D_MODEL), seq3(D_MODEL), sub3(LANES), sub3(LANES), sub3(D_MODEL),
                  small, small, small, small, pl.BlockSpec((1, V_DIM), lambda s: (0, 0))],
        out_specs=seq3(D_MODEL),
        out_shape=jax.ShapeDtypeStruct((b, 1, D_MODEL), BF16),
        compiler_params=_cparams("parallel"),
        name="attn_sample_finish",
    )(q_rows, k_new.reshape(b, 1, D_MODEL), v_new.reshape(b, 1, D_MODEL), m, l, acc,
      *[a.astype(F32).reshape(1, HEAD_DIM) for a in lams], subln.astype(F32).reshape(1, V_DIM))
    return out.reshape(b, D_MODEL)


def _out_proj_body(x_ref, a_ref, w_ref, o_ref):
    o_ref[...] = x_ref[...] + _dot(a_ref[...], w_ref[...])


def _out_proj(x, a, w_o):
    rows = x.shape[0]
    tm = _row_tile(rows)
    row_spec = pl.BlockSpec((tm, D_MODEL), lambda i: (i, 0))
    return pl.pallas_call(
        _out_proj_body,
        grid=(rows // tm,),
        in_specs=[row_spec, row_spec, _resident((D_MODEL, D_MODEL))],
        out_specs=row_spec,
        out_shape=jax.ShapeDtypeStruct((rows, D_MODEL), F32),
        compiler_params=_cparams("parallel"),
        name="out_proj",
    )(x, a, w_o)


def _block_diag_rows(q):
    head = jnp.arange(D_MODEL) // HEAD_DIM
    keep = head[None, :] == jnp.arange(N_SUB)[:, None]
    return jnp.where(keep[None], q[:, None, :], jnp.zeros((), q.dtype))


def kernel(x_prompt, x_sample, state_s5_re, state_s5_im, cache_k, cache_v, page_table, ffn_norm, ffn_w_in, ffn_w_out, mix_norm, s5_lambda_re, s5_lambda_im, s5_log_dt, s5_b_re, s5_b_im, s5_c_re, s5_c_im, s5_d, s5_w_glu, s5_b_glu, attn_w_qkv, attn_q_norm, attn_k_norm, attn_lambda_q1, attn_lambda_k1, attn_lambda_q2, attn_lambda_k2, attn_subln, attn_w_o):
    bsz, seq, _ = x_prompt.shape
    dbs, dseq, _ = x_sample.shape
    assert dseq == 1, "the sample group decodes one token per sequence"
    assert SUBLANES % bsz == 0 and seq % S5_CHUNK == 0
    depth = ffn_norm.shape[0]
    past_len = page_table.shape[1] * cache_k.shape[2]

    xp = x_prompt.reshape(bsz * seq, D_MODEL)
    xs = x_sample.reshape(dbs * dseq, D_MODEL)
    w_in = ffn_w_in.astype(BF16)
    w_out = ffn_w_out.astype(BF16)
    f32n = ffn_norm.astype(F32)
    mixn = mix_norm.astype(F32)

    s5_re_p, s5_im_p, s5_re_s, s5_im_s = [], [], [], []
    k_p, v_p, k_s, v_s = [], [], [], []
    for i in range(depth):
        a = i // N_MIXERS
        if i % N_MIXERS == 0:
            xp, hp = _ffn(xp, f32n[i, 0], w_in[i, 0], w_out[i, 0], mixn[i], F32)
            xs, hs = _ffn(xs, f32n[i, 0], w_in[i, 0], w_out[i, 0], mixn[i], F32)
            prm = (s5_lambda_re[a], s5_lambda_im[a], s5_log_dt[a], s5_b_re[a], s5_b_im[a], s5_c_re[a], s5_c_im[a])
            w_glu = s5_w_glu[a].astype(BF16)
            yp, hr_p, hi_p = _s5_prompt(hp, _s5_chunk_tables(*prm), bsz)
            ys, hr_s, hi_s = _s5_sample(hs, state_s5_re[a], state_s5_im[a], *prm)
            xp = _s5_post(xp, yp, mixn[i], s5_d[a], w_glu, s5_b_glu[a])
            xs = _s5_post(xs, ys, mixn[i], s5_d[a], w_glu, s5_b_glu[a])
            s5_re_p.append(hr_p)
            s5_im_p.append(hi_p)
            s5_re_s.append(hr_s)
            s5_im_s.append(hi_s)
        else:
            lam_init = 0.8 - 0.6 * math.exp(-0.3 * i)
            lams = (attn_lambda_q1[a], attn_lambda_k1[a], attn_lambda_q2[a], attn_lambda_k2[a])
            w_qkv = attn_w_qkv[a].astype(BF16)
            w_o = attn_w_o[a].astype(BF16)
            cos_p, sin_p = _rope_tables(jnp.arange(seq))
            cos_s, sin_s = _rope_tables(jnp.tile(past_len + jnp.arange(dseq), dbs))
            xs, hs = _ffn(xs, f32n[i, 0], w_in[i, 0], w_out[i, 0], mixn[i], BF16)
            qs, ksf, ksb, vsf, vsb = _qkv(hs, w_qkv, attn_q_norm[a], attn_k_norm[a], cos_s, sin_s)
            q_rows = _block_diag_rows(qs)
            ck_t = jnp.transpose(cache_k, (0, 1, 3, 4, 2))
            cv_rows = cache_v.reshape(cache_v.shape[:2] + (cache_v.shape[2] * N_HEADS, V_DIM))
            first = dbs // 2
            xp, hp, m0, l0, acc0 = _ffn_pages(xp, f32n[i, 0], w_in[i, 0], w_out[i, 0], q_rows, ck_t, cv_rows, a,
                                              page_table, 0, first, mixn[i], BF16)
            qp, kpf, kpb, vpf, vpb = _qkv(hp, w_qkv, attn_q_norm[a], attn_k_norm[a], cos_p, sin_p, seq_minor_k=True)
            op = _attn_prompt(qp, kpb, vpb, lams, attn_subln[a], lam_init, bsz)
            xp = _out_proj(xp, op, w_o)
            xp, m1, l1, acc1 = _ffn_pages(xp, f32n[i, 1], w_in[i, 1], w_out[i, 1], q_rows, ck_t, cv_rows, a,
                                          page_table, first, dbs - first)
            m, l, acc = (jnp.concatenate(parts, axis=0) for parts in ((m0, m1), (l0, l1), (acc0, acc1)))
            os_ = _attn_sample_finish(q_rows, ksb, vsb, m, l, acc, lams, attn_subln[a], lam_init)
            xs = _out_proj(xs, os_, w_o)
            xs = _ffn(xs, f32n[i, 1], w_in[i, 1], w_out[i, 1])
            k_p.append(kpf.transpose(0, 3, 1, 2))
            v_p.append(vpf.reshape(bsz, seq, N_HEADS, V_DIM))
            k_s.append(ksf.reshape(dbs, dseq, N_SUB, HEAD_DIM))
            v_s.append(vsf.reshape(dbs, dseq, N_HEADS, V_DIM))
            continue
        xp = _ffn(xp, f32n[i, 1], w_in[i, 1], w_out[i, 1])
        xs = _ffn(xs, f32n[i, 1], w_in[i, 1], w_out[i, 1])
    return (xp.reshape(bsz, seq, D_MODEL), xs.reshape(dbs, dseq, D_MODEL),
            jnp.stack(s5_re_p), jnp.stack(s5_im_p), jnp.stack(s5_re_s), jnp.stack(s5_im_s),
            jnp.stack(k_p), jnp.stack(v_p), jnp.stack(k_s), jnp.stack(v_s))
```

```python
import functools
import math

import jax
import jax.numpy as jnp
from jax import lax
from jax.experimental import pallas as pl
from jax.experimental.pallas import tpu as pltpu

F32 = jnp.float32
BF16 = jnp.bfloat16

D_MODEL = 1024
D_FF = 2816
S5_GROUP = 16
S5_GROUPS = D_MODEL // S5_GROUP
S5_STATE = 64
S5_CHUNK = 16
N_HEADS = 8
N_SUB = 2 * N_HEADS
HEAD_DIM = D_MODEL // N_SUB
V_DIM = 2 * HEAD_DIM
ATTN_SCALE = HEAD_DIM ** -0.5
ROPE_THETA = 10000.0
EPS = 1e-6
NEG_INF = -1e30
N_MIXERS = 2

LANES = 128
SUBLANES = 8
MXU_TILE = 256
FF_CHUNK = MXU_TILE
ROW_TILE = 512
ATTN_Q_TILE = 1024
ATTN_K_TILE = 1024
HOSTED_GROUP_PAGES = 4
HOSTED_RING = 4
LANE_GROUPS = LANES // S5_GROUP
S5_SEQS_PER_STEP = 2
S5_RELAYOUT_ROWS = 64
VMEM_LIMIT = 48 * 1024 * 1024
VMEM_LIMIT_HOSTED = 56 * 1024 * 1024


def _cparams(*sem, vmem=None):
    return pltpu.CompilerParams(dimension_semantics=sem, vmem_limit_bytes=vmem or VMEM_LIMIT)


def _resident(shape):
    zeros = (0,) * len(shape)
    return pl.BlockSpec(shape, lambda *_: zeros, pipeline_mode=pl.Buffered(1))


def _rms(x, g):
    return x * lax.rsqrt(jnp.mean(x * x, axis=-1, keepdims=True) + EPS) * g


def _dot(a, b):
    return jnp.dot(a, b, preferred_element_type=F32)


def _dot_nt(a, b):
    return lax.dot_general(a, b, (((1,), (1,)), ((), ())), preferred_element_type=F32)


def _row_tile(rows):
    return ROW_TILE if rows % ROW_TILE == 0 else rows


def _ffn_body(x_ref, g_ref, win_ref, wout_ref, *rest, norm_dtype):
    if norm_dtype is None:
        o_ref, act_ref = rest
    else:
        gm_ref, o_ref, hn_ref, act_ref = rest
    x = x_ref[...]
    xn = _rms(x, g_ref[...]).astype(BF16)
    for c in range(D_FF // FF_CHUNK):
        lo = c * FF_CHUNK
        g = _dot(xn, win_ref[:, lo:lo + FF_CHUNK])
        u = _dot(xn, win_ref[:, D_FF + lo:D_FF + lo + FF_CHUNK])
        act_ref[:, lo:lo + FF_CHUNK] = (g * jax.nn.sigmoid(g) * u).astype(BF16)
    xo = x + 0.5 * _dot(act_ref[...], wout_ref[...])
    o_ref[...] = xo
    if norm_dtype is not None:
        hn_ref[...] = _rms(xo, gm_ref[...]).astype(norm_dtype)


def _ffn(x, g, w_in, w_out, g_mix=None, norm_dtype=None):
    rows = x.shape[0]
    tm = _row_tile(rows)
    row_spec = pl.BlockSpec((tm, D_MODEL), lambda i: (i, 0))
    in_specs = [row_spec, _resident((1, D_MODEL)), _resident((D_MODEL, 2 * D_FF)), _resident((D_FF, D_MODEL))]
    args = [x, g.reshape(1, D_MODEL), w_in, w_out]
    out_shape = [jax.ShapeDtypeStruct((rows, D_MODEL), F32)]
    out_specs = [row_spec]
    if norm_dtype is not None:
        in_specs.append(_resident((1, D_MODEL)))
        args.append(g_mix.reshape(1, D_MODEL))
        out_shape.append(jax.ShapeDtypeStruct((rows, D_MODEL), norm_dtype))
        out_specs.append(row_spec)
    res = pl.pallas_call(
        functools.partial(_ffn_body, norm_dtype=norm_dtype),
        grid=(rows // tm,),
        in_specs=in_specs,
        out_specs=out_specs,
        out_shape=out_shape,
        scratch_shapes=[pltpu.VMEM((tm, D_FF), BF16)],
        compiler_params=_cparams("parallel"),
        name="ffn_norm" if norm_dtype is not None else "ffn",
    )(*args)
    return res if norm_dtype is not None else res[0]


def _s5_discretise(lam_re, lam_im, log_dt, b_re, b_im):
    dt = jnp.exp(log_dt.astype(F32))[:, None]
    lr = lam_re.astype(F32)
    li = lam_im.astype(F32)
    mag = jnp.exp(lr * dt)
    a_re = mag * jnp.cos(li * dt)
    a_im = mag * jnp.sin(li * dt)
    den = lr * lr + li * li
    nr = a_re - 1.0
    z_re = (nr * lr + a_im * li) / den
    z_im = (a_im * lr - nr * li) / den
    br = b_re.astype(F32)
    bi = b_im.astype(F32)
    bb_re = z_re[..., None] * br - z_im[..., None] * bi
    bb_im = z_re[..., None] * bi + z_im[..., None] * br
    return a_re, a_im, bb_re, bb_im, lr * dt, li * dt


def _s5_chunk_tables(lam_re, lam_im, log_dt, b_re, b_im, c_re, c_im):
    L, G, N, C = S5_CHUNK, S5_GROUPS, S5_STATE, S5_GROUP
    _, _, bb_re, bb_im, ldr, ldi = _s5_discretise(lam_re, lam_im, log_dt, b_re, b_im)
    tau = jnp.arange(L + 1, dtype=F32)[:, None, None]
    mag = jnp.exp(ldr[None] * tau)
    p_re = mag * jnp.cos(ldi[None] * tau)
    p_im = mag * jnp.sin(ldi[None] * tau)
    cr = c_re.astype(F32)
    ci = c_im.astype(F32)
    hi = lax.Precision.HIGHEST
    ca_re = cr[None] * p_re[:, :, None, :] - ci[None] * p_im[:, :, None, :]
    ca_im = cr[None] * p_im[:, :, None, :] + ci[None] * p_re[:, :, None, :]
    ca_rows = jnp.concatenate([ca_re[:L], ca_im[:L]], axis=-1).transpose(1, 3, 0, 2).reshape(G, 2 * N, L * C)
    bb_rows = jnp.concatenate([bb_re, -bb_im], axis=1)
    resp = jnp.einsum('gki,gkx->gix', bb_rows, ca_rows, precision=hi)
    toep = jnp.stack([jnp.pad(resp[:, :, :(L - s) * C], ((0, 0), (0, 0), (s * C, 0))) for s in range(L)], axis=1)
    toep = toep.reshape(G, L * C, L * C)
    pw_re = p_re[L - 1 - jnp.arange(L)]
    pw_im = p_im[L - 1 - jnp.arange(L)]
    w_re = pw_re[..., None] * bb_re[None] - pw_im[..., None] * bb_im[None]
    w_im = pw_re[..., None] * bb_im[None] + pw_im[..., None] * bb_re[None]
    w_re = w_re.transpose(1, 0, 3, 2).reshape(G, L * C, N)
    w_im = w_im.transpose(1, 0, 3, 2).reshape(G, L * C, N)
    w_in = jnp.concatenate([w_re, w_im, w_im, w_re], axis=-1)
    v_re = ca_re[1:].transpose(1, 3, 0, 2).reshape(G, N, L * C)
    v_im = -ca_im[1:].transpose(1, 3, 0, 2).reshape(G, N, L * C)
    v_out = jnp.concatenate([v_re, v_im], axis=1)
    al_re, al_im = p_re[L], p_im[L]
    decay = jnp.stack([jnp.concatenate([al_re, al_re], -1),
                       jnp.concatenate([-al_im, al_im], -1),
                       jnp.concatenate([al_im, -al_im], -1)], axis=1)
    return toep.astype(BF16), w_in.astype(BF16), v_out.astype(BF16), decay


def _block_transpose(xs):
    n = len(xs)
    block = lax.broadcasted_iota(jnp.int32, xs[0].shape, 1) // S5_GROUP
    d = n // 2
    while d >= 1:
        upper = (block // d) % 2 == 1
        nxt = list(xs)
        for i in range(n):
            if i & d == 0:
                a, b = xs[i], xs[i | d]
                nxt[i] = jnp.where(upper, pltpu.roll(b, d * S5_GROUP, 1), a)
                nxt[i | d] = jnp.where(upper, b, pltpu.roll(a, LANES - d * S5_GROUP, 1))
        xs = nxt
        d //= 2
    return xs


def _s5_prompt_body(hn_ref, toep_ref, win_ref, vout_ref, dec_ref, y_ref, hfin_ref, u_scr, sy_scr, hp_scr, *, sps):
    lg = LANE_GROUPS
    rows = u_scr.shape[2]
    nch = rows // sps
    n2 = 2 * S5_STATE
    half = S5_CHUNK // 2
    rb = S5_RELAYOUT_ROWS
    blocks_per_seq = nch // rb

    def chunk_rows(blk):
        seq = blk // blocks_per_seq
        c0 = (blk % blocks_per_seq) * rb
        return c0 * sps + seq, (seq * nch + c0) * S5_CHUNK

    def gather(blk, _):
        r0, t0 = chunk_rows(blk)
        for h in range(2):
            xs = [hn_ref[pl.ds(t0 + h * half + tt, rb, stride=S5_CHUNK), :] for tt in range(half)]
            for g, y in enumerate(_block_transpose(xs)):
                u_scr[g, h, pl.ds(r0, rb, stride=sps), :] = y
        return _

    lax.fori_loop(0, sps * blocks_per_seq, gather, 0)

    def chunk_inputs(g):
        return jnp.concatenate([u_scr[g, 0], u_scr[g, 1]], axis=1).astype(BF16)

    for g in range(lg):
        s = _dot(chunk_inputs(g), win_ref[g])
        sy_scr[g, 0] = s[:, :n2]
        sy_scr[g, 1] = s[:, n2:]
    decs = [dec_ref[g] for g in range(lg)]
    cpv = SUBLANES // sps

    def step(i, carry):
        r0 = pl.multiple_of(i * SUBLANES, SUBLANES)
        new = []
        for g in range(lg):
            h, hs = carry[g]
            a1 = decs[g][0:1, :]
            a2 = decs[g][1:2, :]
            a2s = decs[g][2:3, :]
            slab = sy_scr[g, 0, pl.ds(r0, SUBLANES), :]
            slab_s = sy_scr[g, 1, pl.ds(r0, SUBLANES), :]
            prev = []
            for c in range(cpv):
                prev.append(h)
                rs = slice(c * sps, (c + 1) * sps)
                h, hs = (a1 * h + a2 * hs + slab[rs], a1 * hs + a2s * h + slab_s[rs])
            hp_scr[g, pl.ds(r0, SUBLANES), :] = jnp.concatenate(prev, axis=0)
            new.append((h, hs))
        return tuple(new)

    zero = jnp.zeros((sps, n2), F32)
    fin = lax.fori_loop(0, rows // SUBLANES, step, tuple((zero, zero) for _ in range(lg)))
    for g in range(lg):
        hfin_ref[0, g] = fin[g][0]
        y = _dot(chunk_inputs(g), toep_ref[g]) + _dot(hp_scr[g].astype(BF16), vout_ref[g])
        sy_scr[g, 0] = y[:, :n2]
        sy_scr[g, 1] = y[:, n2:]

    def scatter(blk, _):
        r0, t0 = chunk_rows(blk)
        for h in range(2):
            ys = [sy_scr[g, h, pl.ds(r0, rb, stride=sps), :] for g in range(lg)]
            for tt, x in enumerate(_block_transpose(ys)):
                y_ref[pl.ds(t0 + h * half + tt, rb, stride=S5_CHUNK), :] = x
        return _

    lax.fori_loop(0, sps * blocks_per_seq, scatter, 0)


def _s5_prompt(hn, tables, bsz):
    toep, w_in, v_out, decay = tables
    G, L, N = S5_GROUPS, S5_CHUNK, S5_STATE
    t = hn.shape[0] // bsz
    sps = S5_SEQS_PER_STEP
    assert bsz % sps == 0 and SUBLANES % sps == 0 and (t // L) % S5_RELAYOUT_ROWS == 0
    rows = (t // L) * sps
    lg = LANE_GROUPS
    lc = L * S5_GROUP
    tok_spec = pl.BlockSpec((sps * t, LANES), lambda j, s: (s, j))
    gspec = lambda *shape: pl.BlockSpec((lg,) + shape, lambda j, s: (j,) + (0,) * len(shape))
    y, hfin = pl.pallas_call(
        functools.partial(_s5_prompt_body, sps=sps),
        grid=(G // lg, bsz // sps),
        in_specs=[tok_spec, gspec(lc, lc), gspec(lc, lc), gspec(2 * N, lc), gspec(3, 2 * N)],
        out_specs=[tok_spec, pl.BlockSpec((1, lg, sps, 2 * N), lambda j, s: (s, j, 0, 0))],
        out_shape=[jax.ShapeDtypeStruct(hn.shape, F32), jax.ShapeDtypeStruct((bsz // sps, G, sps, 2 * N), F32)],
        scratch_shapes=[pltpu.VMEM((lg, 2, rows, LANES), F32), pltpu.VMEM((lg, 2, rows, LANES), F32),
                        pltpu.VMEM((lg, rows, 2 * N), F32)],
        compiler_params=_cparams("parallel", "parallel"),
        name="s5_prompt",
    )(hn, toep, w_in, v_out, decay)
    hfin = hfin.transpose(0, 2, 1, 3).reshape(bsz, G, 2 * N)
    return y, hfin[:, :, :N], hfin[:, :, N:]


def _s5_sample_body(u_ref, bbr_ref, bbi_ref, ar_ref, ai_ref, h0r_ref, h0i_ref, cr_ref, ci_ref,
                    y_ref, hr_ref, hi_ref):
    hi = lax.Precision.HIGHEST
    u = u_ref[...]
    bu_re = jnp.einsum('gbc,gnc->gbn', u, bbr_ref[...], precision=hi, preferred_element_type=F32)
    bu_im = jnp.einsum('gbc,gnc->gbn', u, bbi_ref[...], precision=hi, preferred_element_type=F32)
    ar, ai = ar_ref[...], ai_ref[...]
    h0r, h0i = h0r_ref[...], h0i_ref[...]
    hr = bu_re + ar * h0r - ai * h0i
    hi_ = bu_im + ar * h0i + ai * h0r
    hr_ref[...] = hr
    hi_ref[...] = hi_
    y_ref[...] = (jnp.einsum('gbn,gcn->gbc', hr, cr_ref[...], precision=hi, preferred_element_type=F32)
                  - jnp.einsum('gbn,gcn->gbc', hi_, ci_ref[...], precision=hi, preferred_element_type=F32))


def _s5_sample(hs, h0_re, h0_im, lam_re, lam_im, log_dt, b_re, b_im, c_re, c_im):
    G, C, N = S5_GROUPS, S5_GROUP, S5_STATE
    b = hs.shape[0]
    a_re, a_im, bb_re, bb_im, _, _ = _s5_discretise(lam_re, lam_im, log_dt, b_re, b_im)
    u = hs.reshape(b, G, C).transpose(1, 0, 2)
    args = [u, bb_re, bb_im, a_re[:, None, :], a_im[:, None, :],
            h0_re.astype(F32).transpose(1, 0, 2), h0_im.astype(F32).transpose(1, 0, 2),
            c_re.astype(F32), c_im.astype(F32)]
    full = lambda a: pl.BlockSpec(a.shape, lambda i: (0,) * a.ndim)
    outs = [jax.ShapeDtypeStruct((G, b, C), F32), jax.ShapeDtypeStruct((G, b, N), F32),
            jax.ShapeDtypeStruct((G, b, N), F32)]
    y, hr, hi = pl.pallas_call(
        _s5_sample_body,
        grid=(1,),
        in_specs=[full(a) for a in args],
        out_specs=[full(o) for o in outs],
        out_shape=outs,
        compiler_params=_cparams("arbitrary"),
        name="s5_sample",
    )(*args)
    return y.transpose(1, 0, 2).reshape(b, D_MODEL), hr.transpose(1, 0, 2), hi.transpose(1, 0, 2)


def _s5_post_body(x_ref, y_ref, gm_ref, d_ref, w_ref, b_ref, o_ref):
    x = x_ref[...]
    hp = _rms(x, gm_ref[...])
    z = jax.nn.gelu(y_ref[...] + d_ref[...] * hp)
    zz = _dot(z.astype(BF16), w_ref[...]) + b_ref[...]
    o_ref[...] = x + zz[:, :D_MODEL] * jax.nn.sigmoid(zz[:, D_MODEL:])


def _s5_post(x, y, g_mix, d, w_glu, b_glu):
    rows = x.shape[0]
    tm = _row_tile(rows)
    row_spec = pl.BlockSpec((tm, D_MODEL), lambda i: (i, 0))
    return pl.pallas_call(
        _s5_post_body,
        grid=(rows // tm,),
        in_specs=[row_spec, row_spec, _resident((1, D_MODEL)), _resident((1, D_MODEL)),
                  _resident((D_MODEL, 2 * D_MODEL)), _resident((1, 2 * D_MODEL))],
        out_specs=row_spec,
        out_shape=jax.ShapeDtypeStruct((rows, D_MODEL), F32),
        compiler_params=_cparams("parallel"),
        name="s5_post",
    )(x, y, g_mix.reshape(1, D_MODEL), d.astype(F32).reshape(1, D_MODEL), w_glu, b_glu.astype(F32).reshape(1, -1))


def _rope_tables(pos):
    half = HEAD_DIM // 2
    inv = ROPE_THETA ** (-jnp.arange(half, dtype=F32) / half)
    ang = pos.astype(F32)[:, None] * inv[None, :]
    cos = jnp.cos(ang)
    sin = jnp.sin(ang)
    cos_t = jnp.concatenate([cos, cos, cos, cos], axis=-1)
    sin_t = jnp.concatenate([-sin, sin, -sin, sin], axis=-1)
    return cos_t, sin_t


def _head_sum_matrix():
    head = jnp.arange(D_MODEL) // HEAD_DIM
    return (head[:, None] == head[None, :]).astype(BF16)


def _qkv_body(hn_ref, w_ref, qg_ref, kg_ref, hsum_ref, cos_ref, sin_ref,
              q_ref, kf_ref, kb_ref, vf_ref, vb_ref, *, seq_minor_k):
    hn = hn_ref[...]
    cos = cos_ref[...]
    sin = sin_ref[...]
    lane = lax.broadcasted_iota(jnp.int32, cos.shape, 1)
    low_half = (lane % HEAD_DIM) < (HEAD_DIM // 2)

    def norm_rope(x, g):
        ss = _dot((x * x).astype(BF16), hsum_ref[...])
        y = x * lax.rsqrt(ss * (1.0 / HEAD_DIM) + EPS) * g
        outs = []
        for j in range(D_MODEL // LANES):
            t = y[:, j * LANES:(j + 1) * LANES]
            below = pltpu.roll(t, HEAD_DIM // 2, 1)
            above = pltpu.roll(t, LANES - HEAD_DIM // 2, 1)
            outs.append(t * cos + jnp.where(low_half, above, below) * sin)
        return jnp.concatenate(outs, axis=1)

    q = norm_rope(_dot(hn, w_ref[:, 0:D_MODEL]), qg_ref[...])
    q_ref[...] = (q * ATTN_SCALE).astype(BF16)
    k = norm_rope(_dot(hn, w_ref[:, D_MODEL:2 * D_MODEL]), kg_ref[...])
    if seq_minor_k:
        kf_ref[0] = k.T.reshape(kf_ref.shape[1:])
    else:
        kf_ref[...] = k
    kb_ref[...] = k.astype(BF16)
    v = _dot(hn, w_ref[:, 2 * D_MODEL:3 * D_MODEL])
    vf_ref[...] = v
    vb_ref[...] = v.astype(BF16)


def _qkv(hn, w_qkv, q_norm, k_norm, cos_t, sin_t, seq_minor_k=False):
    rows = hn.shape[0]
    tm = _row_tile(rows)
    tab_blocks = cos_t.shape[0] // tm
    row_spec = pl.BlockSpec((tm, D_MODEL), lambda i: (i, 0))
    tab_spec = pl.BlockSpec((tm, LANES), lambda i: (i % tab_blocks, 0))
    gain = lambda g: jnp.tile(g.astype(F32), N_SUB).reshape(1, D_MODEL)
    outs = [jax.ShapeDtypeStruct((rows, D_MODEL), dt) for dt in (BF16, F32, BF16, F32, BF16)]
    out_specs = [row_spec] * 5
    if seq_minor_k:
        period = cos_t.shape[0]
        outs[1] = jax.ShapeDtypeStruct((rows // period, N_SUB, HEAD_DIM, period), F32)
        out_specs[1] = pl.BlockSpec((1, N_SUB, HEAD_DIM, tm), lambda i: (i // tab_blocks, 0, 0, i % tab_blocks))
    return pl.pallas_call(
        functools.partial(_qkv_body, seq_minor_k=seq_minor_k),
        grid=(rows // tm,),
        in_specs=[row_spec, _resident((D_MODEL, 3 * D_MODEL)), _resident((1, D_MODEL)), _resident((1, D_MODEL)),
                  _resident((D_MODEL, D_MODEL)), tab_spec, tab_spec],
        out_specs=out_specs,
        out_shape=outs,
        compiler_params=_cparams("parallel"),
        name="qkv",
    )(hn, w_qkv, gain(q_norm), gain(k_norm), _head_sum_matrix(), cos_t, sin_t)


def _lambda_full(lq1_ref, lk1_ref, lq2_ref, lk2_ref, lam_init):
    s1 = jnp.sum(lq1_ref[...] * lk1_ref[...], axis=-1, keepdims=True)
    s2 = jnp.sum(lq2_ref[...] * lk2_ref[...], axis=-1, keepdims=True)
    return jnp.exp(s1) - jnp.exp(s2) + lam_init


def _merge(o1, o2, lam, subln, lam_init):
    out = o1 - lam * o2
    return _rms(out, subln) * (1.0 - lam_init)


def _attn_prompt_body(q_ref, k_ref, v_ref, lq1_ref, lk1_ref, lq2_ref, lk2_ref, subln_ref, o_ref, *, lam_init, tk, nq):
    tq = q_ref.shape[0]
    qi = pl.program_id(2)
    kt_per_q = tq // tk
    qt = q_ref[...].astype(F32).T.astype(BF16)
    dim = lax.broadcasted_iota(jnp.int32, qt.shape, 0)
    zero = jnp.zeros_like(qt)
    qst = jnp.concatenate([jnp.where(dim < HEAD_DIM, qt, zero), jnp.where(dim >= HEAD_DIM, qt, zero)], axis=1)

    span = MXU_TILE
    n_span = 2 * tq // span

    def visible_keys(c, diagonal):
        if diagonal is None:
            return tk
        return max(0, min(tk, (c * span) % tq + span - diagonal * tk))

    def scores(j, diagonal):
        out = []
        for c in range(n_span):
            nk = visible_keys(c, diagonal)
            out.append(_dot(k_ref[pl.ds(j * tk, nk), :], qst[:, c * span:(c + 1) * span]) if nk else None)
        return out

    def absorb(j, tile_scores, carry, diagonal):
        new = []
        for c, (s, (m, l, acc)) in enumerate(zip(tile_scores, carry)):
            if s is None:
                new.append((m, l, acc))
                continue
            if diagonal is not None:
                key = lax.broadcasted_iota(jnp.int32, s.shape, 0) + diagonal * tk
                qry = lax.broadcasted_iota(jnp.int32, s.shape, 1) + (c * span) % tq
                s = jnp.where(key <= qry, s, NEG_INF)
            m_new = jnp.maximum(m, jnp.max(s, axis=0, keepdims=True))
            alpha = jnp.exp(m - m_new)
            p = jnp.exp(s - m_new)
            l_new = l * alpha + jnp.sum(p, axis=0, keepdims=True)
            vj = v_ref[pl.ds(j * tk, s.shape[0]), :]
            pv = lax.dot_general(vj, p.astype(BF16), (((0,), (0,)), ((), ())), preferred_element_type=F32)
            new.append((m_new, l_new, acc * alpha + pv))
        return new

    def run(n_tiles):
        carry = [(jnp.full((1, span), NEG_INF, F32), jnp.zeros((1, span), F32), jnp.zeros((V_DIM, span), F32))
                 for _ in range(n_span)]
        def diagonal(j):
            d = j - (n_tiles - kt_per_q)
            return d if d >= 0 else None

        tile_scores = scores(0, diagonal(0))
        for j in range(n_tiles):
            next_scores = scores(j + 1, diagonal(j + 1)) if j + 1 < n_tiles else None
            carry = absorb(j, tile_scores, carry, diagonal(j))
            tile_scores = next_scores
        o = jnp.concatenate([acc / l for _, l, acc in carry], axis=1)
        lam = _lambda_full(lq1_ref, lk1_ref, lq2_ref, lk2_ref, lam_init)
        out = o[:, :tq] - lam * o[:, tq:]
        out = out * lax.rsqrt(jnp.mean(out * out, axis=0, keepdims=True) + EPS) * subln_ref[...] * (1.0 - lam_init)
        o_ref[...] = out.T.astype(BF16)

    for n in range(nq):
        pl.when(qi == n)(functools.partial(run, (n + 1) * kt_per_q))


def _attn_prompt(q, k, v, lams, subln, lam_init, bsz):
    rows = q.shape[0]
    t = rows // bsz
    tq = ATTN_Q_TILE if t % ATTN_Q_TILE == 0 else t
    tk = ATTN_K_TILE if tq % ATTN_K_TILE == 0 else tq
    nq = t // tq
    small = pl.BlockSpec((1, HEAD_DIM), lambda b, h, i: (0, 0))
    return pl.pallas_call(
        functools.partial(_attn_prompt_body, lam_init=lam_init, tk=tk, nq=nq),
        grid=(bsz, N_HEADS, nq),
        in_specs=[pl.BlockSpec((tq, V_DIM), lambda b, h, i: (b * nq + i, h)),
                  pl.BlockSpec((t, V_DIM), lambda b, h, i: (b, h)),
                  pl.BlockSpec((t, V_DIM), lambda b, h, i: (b, h)),
                  small, small, small, small,
                  pl.BlockSpec((V_DIM, 1), lambda b, h, i: (0, 0))],
        out_specs=pl.BlockSpec((tq, V_DIM), lambda b, h, i: (b * nq + i, h)),
        out_shape=jax.ShapeDtypeStruct((rows, D_MODEL), BF16),
        compiler_params=_cparams("parallel", "parallel", "arbitrary"),
        name="attn_prompt",
    )(q, k, v, *[a.astype(F32).reshape(1, HEAD_DIM) for a in lams], subln.astype(F32).reshape(V_DIM, 1))


def _pages_scores(qr, keys_t, n):
    return _dot(qr, jnp.concatenate([keys_t(i) for i in range(n)], axis=1))


def _pages_absorb(s, values, n, m_ref, l_ref, acc_ref):
    m_old = m_ref[0][:, 0:1]
    l_old = l_ref[0][:, 0:1]
    m_new = jnp.maximum(m_old, jnp.max(s, axis=-1, keepdims=True))
    alpha = jnp.exp(m_old - m_new)
    p = jnp.exp(s - m_new)
    l_new = l_old * alpha + jnp.sum(p, axis=-1, keepdims=True)
    pb = p.astype(BF16)
    pv = _dot(pb, jnp.concatenate([values(i) for i in range(n)], axis=0))
    acc_ref[0] = acc_ref[0] * alpha + pv
    m_ref[0] = jnp.broadcast_to(m_new, m_ref.shape[1:])
    l_ref[0] = jnp.broadcast_to(l_new, l_ref.shape[1:])


def _ffn_pages_body(pt_ref, x_ref, g_ref, win_ref, wout_ref, *rest, norm_dtype, layer, seq0, steps_per_seq, groups):
    if norm_dtype is None:
        qr_ref, ck_ref, cv_ref, o_ref, m_ref, l_ref, acc_ref, act_ref, kbuf, vbuf, sem = rest
    else:
        gm_ref, qr_ref, ck_ref, cv_ref, o_ref, hn_ref, m_ref, l_ref, acc_ref, act_ref, kbuf, vbuf, sem = rest
    i = pl.program_id(0)
    last = pl.num_programs(0) - 1
    gp = HOSTED_GROUP_PAGES
    page = kbuf.shape[-1]

    ring = kbuf.shape[0]
    KEYS, VALUES = 0, 1
    ahead = {KEYS: ring, VALUES: ring - 1}

    def copies(kind, step, grp):
        seq = seq0 + step // steps_per_seq
        first = (step % steps_per_seq) * (groups * gp) + grp * gp
        src, dst = (ck_ref, kbuf) if kind == KEYS else (cv_ref, vbuf)
        slot = grp % ring
        return [pltpu.make_async_copy(src.at[layer, pt_ref[seq, first + pg]], dst.at[slot, pg], sem.at[slot, kind])
                for pg in range(gp)]

    def start(kind, grp):
        if grp < groups:
            for c in copies(kind, i, grp):
                c.start()
        else:
            @pl.when(i < last)
            def _():
                for c in copies(kind, i + 1, grp - groups):
                    c.start()

    def wait(kind, grp):
        for c in copies(kind, i, grp):
            c.wait()

    @pl.when(i == 0)
    def _():
        for kind in (KEYS, VALUES):
            for grp in range(ahead[kind]):
                start(kind, grp)

    @pl.when(i % steps_per_seq == 0)
    def _():
        m_ref[...] = jnp.full(m_ref.shape, NEG_INF, F32)
        l_ref[...] = jnp.zeros(l_ref.shape, F32)
        acc_ref[...] = jnp.zeros(acc_ref.shape, F32)

    x = x_ref[...]
    xn = _rms(x, g_ref[...]).astype(BF16)
    qr = qr_ref[0]
    n_chunks = D_FF // FF_CHUNK

    def scores(grp):
        slot = grp % ring
        return _pages_scores(qr, lambda pg: kbuf[slot, pg].reshape(D_MODEL, page).astype(BF16), gp)

    def values(grp):
        slot = grp % ring

        def page_values(pg):
            heads = [vbuf[slot, pg, pl.ds(h, page, stride=N_HEADS), :] for h in range(N_HEADS)]
            return jnp.concatenate(heads, axis=1).astype(BF16)
        return page_values

    wait(KEYS, 0)
    s = scores(0)
    for grp in range(groups):
        start(KEYS, grp + ahead[KEYS])
        start(VALUES, grp + ahead[VALUES])
        if grp + 1 < groups:
            wait(KEYS, grp + 1)
        wait(VALUES, grp)
        for c in range(grp * n_chunks // groups, (grp + 1) * n_chunks // groups):
            lo = c * FF_CHUNK
            g = _dot(xn, win_ref[:, lo:lo + FF_CHUNK])
            u = _dot(xn, win_ref[:, D_FF + lo:D_FF + lo + FF_CHUNK])
            act_ref[:, lo:lo + FF_CHUNK] = (g * jax.nn.sigmoid(g) * u).astype(BF16)
        s_next = scores(grp + 1) if grp + 1 < groups else None
        _pages_absorb(s, values(grp), gp, m_ref, l_ref, acc_ref)
        s = s_next
    xo = x + 0.5 * _dot(act_ref[...], wout_ref[...])
    o_ref[...] = xo
    if norm_dtype is not None:
        hn_ref[...] = _rms(xo, gm_ref[...]).astype(norm_dtype)


def _ffn_pages(x, g, w_in, w_out, q_rows, cache_k_t, cache_v_rows, layer, page_table, seq0, n_seq,
               g_mix=None, norm_dtype=None):
    rows = x.shape[0]
    tm = ROW_TILE
    steps = rows // tm
    n_pages = page_table.shape[1]
    page = cache_k_t.shape[-1]
    assert rows % tm == 0 and steps % n_seq == 0
    steps_per_seq = steps // n_seq
    groups = n_pages // steps_per_seq // HOSTED_GROUP_PAGES
    assert groups * HOSTED_GROUP_PAGES * steps_per_seq == n_pages and groups % HOSTED_RING == 0
    row_spec = pl.BlockSpec((tm, D_MODEL), lambda i, pt: (i, 0))
    res = lambda shape: pl.BlockSpec(shape, lambda i, pt: (0,) * len(shape), pipeline_mode=pl.Buffered(1))
    seq_spec = lambda w, off: pl.BlockSpec((1, N_SUB, w), lambda i, pt: (off + i // steps_per_seq, 0, 0))
    hbm = pl.BlockSpec(memory_space=pl.ANY)
    in_specs = [row_spec, res((1, D_MODEL)), res((D_MODEL, 2 * D_FF)), res((D_FF, D_MODEL))]
    args = [x, g.reshape(1, D_MODEL), w_in, w_out]
    out_shape = [jax.ShapeDtypeStruct((rows, D_MODEL), F32)]
    out_specs = [row_spec]
    if norm_dtype is not None:
        in_specs.append(res((1, D_MODEL)))
        args.append(g_mix.reshape(1, D_MODEL))
        out_shape.append(jax.ShapeDtypeStruct((rows, D_MODEL), norm_dtype))
        out_specs.append(row_spec)
    in_specs += [seq_spec(D_MODEL, seq0), hbm, hbm]
    args += [q_rows, cache_k_t, cache_v_rows]
    out_shape += [jax.ShapeDtypeStruct((n_seq, N_SUB, LANES), F32), jax.ShapeDtypeStruct((n_seq, N_SUB, LANES), F32),
                  jax.ShapeDtypeStruct((n_seq, N_SUB, D_MODEL), F32)]
    out_specs += [seq_spec(LANES, 0), seq_spec(LANES, 0), seq_spec(D_MODEL, 0)]
    gpp = HOSTED_GROUP_PAGES
    grid_spec = pltpu.PrefetchScalarGridSpec(
        num_scalar_prefetch=1,
        grid=(steps,),
        in_specs=in_specs,
        out_specs=out_specs,
        scratch_shapes=[pltpu.VMEM((tm, D_FF), BF16),
                        pltpu.VMEM((HOSTED_RING, gpp, N_SUB, HEAD_DIM, page), F32),
                        pltpu.VMEM((HOSTED_RING, gpp, page * N_HEADS, V_DIM), F32),
                        pltpu.SemaphoreType.DMA((HOSTED_RING, 2))],
    )
    return pl.pallas_call(
        functools.partial(_ffn_pages_body, norm_dtype=norm_dtype, layer=layer, seq0=seq0,
                          steps_per_seq=steps_per_seq, groups=groups),
        grid_spec=grid_spec,
        out_shape=out_shape,
        compiler_params=_cparams("arbitrary", vmem=VMEM_LIMIT_HOSTED),
        name="ffn_pages_norm" if norm_dtype is not None else "ffn_pages",
    )(page_table, *args)


def _attn_sample_finish_body(qr_ref, kn_ref, vn_ref, m_ref, l_ref, acc_ref,
                             lq1_ref, lk1_ref, lq2_ref, lk2_ref, subln_ref, o_ref, *, lam_init):
    qr = qr_ref[0].astype(F32)
    kn = kn_ref[0].astype(F32)
    vn = vn_ref[0].astype(F32)
    s = jnp.sum(qr * kn, axis=-1, keepdims=True)
    m_old = m_ref[0][:, 0:1]
    l_old = l_ref[0][:, 0:1]
    m_new = jnp.maximum(m_old, s)
    alpha = jnp.exp(m_old - m_new)
    p = jnp.exp(s - m_new)
    l_new = l_old * alpha + p
    acc = acc_ref[0] * alpha + p.astype(BF16).astype(F32) * vn
    o = acc / l_new
    lam = _lambda_full(lq1_ref, lk1_ref, lq2_ref, lk2_ref, lam_init)
    heads = []
    for h in range(N_HEADS):
        lanes = slice(h * V_DIM, (h + 1) * V_DIM)
        heads.append(_merge(o[2 * h:2 * h + 1, lanes], o[2 * h + 1:2 * h + 2, lanes], lam, subln_ref[...], lam_init))
    o_ref[0] = jnp.concatenate(heads, axis=1).astype(BF16)


def _attn_sample_finish(q_rows, k_new, v_new, m, l, acc, lams, subln, lam_init):
    b = q_rows.shape[0]
    seq3 = lambda w: pl.BlockSpec((1, 1, w), lambda s: (s, 0, 0))
    sub3 = lambda w: pl.BlockSpec((1, N_SUB, w), lambda s: (s, 0, 0))
    small = pl.BlockSpec((1, HEAD_DIM), lambda s: (0, 0))
    out = pl.pallas_call(
        functools.partial(_attn_sample_finish_body, lam_init=lam_init),
        grid=(b,),
        in_specs=[sub3(D_MODEL), seq3(D_MODEL), seq3(D_MODEL), sub3(LANES), sub3(LANES), sub3(D_MODEL),
                  small, small, small, small, pl.BlockSpec((1, V_DIM), lambda s: (0, 0))],
        out_specs=seq3(D_MODEL),
        out_shape=jax.ShapeDtypeStruct((b, 1, D_MODEL), BF16),
        compiler_params=_cparams("parallel"),
        name="attn_sample_finish",
    )(q_rows, k_new.reshape(b, 1, D_MODEL), v_new.reshape(b, 1, D_MODEL), m, l, acc,
      *[a.astype(F32).reshape(1, HEAD_DIM) for a in lams], subln.astype(F32).reshape(1, V_DIM))
    return out.reshape(b, D_MODEL)


def _out_proj_body(x_ref, a_ref, w_ref, o_ref):
    o_ref[...] = x_ref[...] + _dot(a_ref[...], w_ref[...])


def _out_proj(x, a, w_o):
    rows = x.shape[0]
    tm = _row_tile(rows)
    row_spec = pl.BlockSpec((tm, D_MODEL), lambda i: (i, 0))
    return pl.pallas_call(
        _out_proj_body,
        grid=(rows // tm,),
        in_specs=[row_spec, row_spec, _resident((D_MODEL, D_MODEL))],
        out_specs=row_spec,
        out_shape=jax.ShapeDtypeStruct((rows, D_MODEL), F32),
        compiler_params=_cparams("parallel"),
        name="out_proj",
    )(x, a, w_o)


def _block_diag_rows(q):
    head = jnp.arange(D_MODEL) // HEAD_DIM
    keep = head[None, :] == jnp.arange(N_SUB)[:, None]
    return jnp.where(keep[None], q[:, None, :], jnp.zeros((), q.dtype))


def kernel(x_prompt, x_sample, state_s5_re, state_s5_im, cache_k, cache_v, page_table, ffn_norm, ffn_w_in, ffn_w_out, mix_norm, s5_lambda_re, s5_lambda_im, s5_log_dt, s5_b_re, s5_b_im, s5_c_re, s5_c_im, s5_d, s5_w_glu, s5_b_glu, attn_w_qkv, attn_q_norm, attn_k_norm, attn_lambda_q1, attn_lambda_k1, attn_lambda_q2, attn_lambda_k2, attn_subln, attn_w_o):
    bsz, seq, _ = x_prompt.shape
    dbs, dseq, _ = x_sample.shape
    assert dseq == 1, "the sample group decodes one token per sequence"
    assert SUBLANES % bsz == 0 and seq % S5_CHUNK == 0
    depth = ffn_norm.shape[0]
    past_len = page_table.shape[1] * cache_k.shape[2]

    xp = x_prompt.reshape(bsz * seq, D_MODEL)
    xs = x_sample.reshape(dbs * dseq, D_MODEL)
    w_in = ffn_w_in.astype(BF16)
    w_out = ffn_w_out.astype(BF16)
    f32n = ffn_norm.astype(F32)
    mixn = mix_norm.astype(F32)

    s5_re_p, s5_im_p, s5_re_s, s5_im_s = [], [], [], []
    k_p, v_p, k_s, v_s = [], [], [], []
    for i in range(depth):
        a = i // N_MIXERS
        if i % N_MIXERS == 0:
            xp, hp = _ffn(xp, f32n[i, 0], w_in[i, 0], w_out[i, 0], mixn[i], F32)
            xs, hs = _ffn(xs, f32n[i, 0], w_in[i, 0], w_out[i, 0], mixn[i], F32)
            prm = (s5_lambda_re[a], s5_lambda_im[a], s5_log_dt[a], s5_b_re[a], s5_b_im[a], s5_c_re[a], s5_c_im[a])
            w_glu = s5_w_glu[a].astype(BF16)
            yp, hr_p, hi_p = _s5_prompt(hp, _s5_chunk_tables(*prm), bsz)
            ys, hr_s, hi_s = _s5_sample(hs, state_s5_re[a], state_s5_im[a], *prm)
            xp = _s5_post(xp, yp, mixn[i], s5_d[a], w_glu, s5_b_glu[a])
            xs = _s5_post(xs, ys, mixn[i], s5_d[a], w_glu, s5_b_glu[a])
            s5_re_p.append(hr_p)
            s5_im_p.append(hi_p)
            s5_re_s.append(hr_s)
            s5_im_s.append(hi_s)
        else:
            lam_init = 0.8 - 0.6 * math.exp(-0.3 * i)
            lams = (attn_lambda_q1[a], attn_lambda_k1[a], attn_lambda_q2[a], attn_lambda_k2[a])
            w_qkv = attn_w_qkv[a].astype(BF16)
            w_o = attn_w_o[a].astype(BF16)
            cos_p, sin_p = _rope_tables(jnp.arange(seq))
            cos_s, sin_s = _rope_tables(jnp.tile(past_len + jnp.arange(dseq), dbs))
            xs, hs = _ffn(xs, f32n[i, 0], w_in[i, 0], w_out[i, 0], mixn[i], BF16)
            qs, ksf, ksb, vsf, vsb = _qkv(hs, w_qkv, attn_q_norm[a], attn_k_norm[a], cos_s, sin_s)
            q_rows = _block_diag_rows(qs)
            ck_t = jnp.transpose(cache_k, (0, 1, 3, 4, 2))
            cv_rows = cache_v.reshape(cache_v.shape[:2] + (cache_v.shape[2] * N_HEADS, V_DIM))
            first = dbs // 2
            xp, hp, m0, l0, acc0 = _ffn_pages(xp, f32n[i, 0], w_in[i, 0], w_out[i, 0], q_rows, ck_t, cv_rows, a,
                                              page_table, 0, first, mixn[i], BF16)
            qp, kpf, kpb, vpf, vpb = _qkv(hp, w_qkv, attn_q_norm[a], attn_k_norm[a], cos_p, sin_p, seq_minor_k=True)
            op = _attn_prompt(qp, kpb, vpb, lams, attn_subln[a], lam_init, bsz)
            xp = _out_proj(xp, op, w_o)
            xp, m1, l1, acc1 = _ffn_pages(xp, f32n[i, 1], w_in[i, 1], w_out[i, 1], q_rows, ck_t, cv_rows, a,
                                          page_table, first, dbs - first)
            m, l, acc = (jnp.concatenate(parts, axis=0) for parts in ((m0, m1), (l0, l1), (acc0, acc1)))
            os_ = _attn_sample_finish(q_rows, ksb, vsb, m, l, acc, lams, attn_subln[a], lam_init)
            xs = _out_proj(xs, os_, w_o)
            xs = _ffn(xs, f32n[i, 1], w_in[i, 1], w_out[i, 1])
            k_p.append(kpf.transpose(0, 3, 1, 2))
            v_p.append(vpf.reshape(bsz, seq, N_HEADS, V_DIM))
            k_s.append(ksf.reshape(dbs, dseq, N_SUB, HEAD_DIM))
            v_s.append(vsf.reshape(dbs, dseq, N_HEADS, V_DIM))
            continue
        xp = _ffn(xp, f32n[i, 1], w_in[i, 1], w_out[i, 1])
        xs = _ffn(xs, f32n[i, 1], w_in[i, 1], w_out[i, 1])
    return (xp.reshape(bsz, seq, D_MODEL), xs.reshape(dbs, dseq, D_MODEL),
            jnp.stack(s5_re_p), jnp.stack(s5_im_p), jnp.stack(s5_re_s), jnp.stack(s5_im_s),
            jnp.stack(k_p), jnp.stack(v_p), jnp.stack(k_s), jnp.stack(v_s))
```

```python
import functools
import math

import jax
import jax.numpy as jnp
from jax import lax
from jax.experimental import pallas as pl
from jax.experimental.pallas import tpu as pltpu

F32 = jnp.float32
BF16 = jnp.bfloat16

D_MODEL = 1024
D_FF = 2816
S5_GROUP = 16
S5_GROUPS = D_MODEL // S5_GROUP
S5_STATE = 64
S5_CHUNK = 16
N_HEADS = 8
N_SUB = 2 * N_HEADS
HEAD_DIM = D_MODEL // N_SUB
V_DIM = 2 * HEAD_DIM
ATTN_SCALE = HEAD_DIM ** -0.5
ROPE_THETA = 10000.0
EPS = 1e-6
NEG_INF = -1e30
N_MIXERS = 2

LANES = 128
SUBLANES = 8
MXU_TILE = 256
FF_CHUNK = MXU_TILE
ROW_TILE = 512
ATTN_Q_TILE = 1024
ATTN_K_TILE = 1024
HOSTED_GROUP_PAGES = 4
HOSTED_RING = 4
LANE_GROUPS = LANES // S5_GROUP
S5_SEQS_PER_STEP = 2
S5_RELAYOUT_ROWS = 64
VMEM_LIMIT = 48 * 1024 * 1024
VMEM_LIMIT_HOSTED = 56 * 1024 * 1024


def _cparams(*sem, vmem=None):
    return pltpu.CompilerParams(dimension_semantics=sem, vmem_limit_bytes=vmem or VMEM_LIMIT)


def _resident(shape):
    zeros = (0,) * len(shape)
    return pl.BlockSpec(shape, lambda *_: zeros, pipeline_mode=pl.Buffered(1))


def _rms(x, g):
    return x * lax.rsqrt(jnp.mean(x * x, axis=-1, keepdims=True) + EPS) * g


def _dot(a, b):
    return jnp.dot(a, b, preferred_element_type=F32)


def _dot_nt(a, b):
    return lax.dot_general(a, b, (((1,), (1,)), ((), ())), preferred_element_type=F32)


def _row_tile(rows):
    return ROW_TILE if rows % ROW_TILE == 0 else rows


def _ffn_body(x_ref, g_ref, win_ref, wout_ref, *rest, norm_dtype):
    if norm_dtype is None:
        o_ref, act_ref = rest
    else:
        gm_ref, o_ref, hn_ref, act_ref = rest
    x = x_ref[...]
    xn = _rms(x, g_ref[...]).astype(BF16)
    for c in range(D_FF // FF_CHUNK):
        lo = c * FF_CHUNK
        g = _dot(xn, win_ref[:, lo:lo + FF_CHUNK])
        u = _dot(xn, win_ref[:, D_FF + lo:D_FF + lo + FF_CHUNK])
        act_ref[:, lo:lo + FF_CHUNK] = (g * jax.nn.sigmoid(g) * u).astype(BF16)
    xo = x + 0.5 * _dot(act_ref[...], wout_ref[...])
    o_ref[...] = xo
    if norm_dtype is not None:
        hn_ref[...] = _rms(xo, gm_ref[...]).astype(norm_dtype)


def _ffn(x, g, w_in, w_out, g_mix=None, norm_dtype=None):
    rows = x.shape[0]
    tm = _row_tile(rows)
    row_spec = pl.BlockSpec((tm, D_MODEL), lambda i: (i, 0))
    in_specs = [row_spec, _resident((1, D_MODEL)), _resident((D_MODEL, 2 * D_FF)), _resident((D_FF, D_MODEL))]
    args = [x, g.reshape(1, D_MODEL), w_in, w_out]
    out_shape = [jax.ShapeDtypeStruct((rows, D_MODEL), F32)]
    out_specs = [row_spec]
    if norm_dtype is not None:
        in_specs.append(_resident((1, D_MODEL)))
        args.append(g_mix.reshape(1, D_MODEL))
        out_shape.append(jax.ShapeDtypeStruct((rows, D_MODEL), norm_dtype))
        out_specs.append(row_spec)
    res = pl.pallas_call(
        functools.partial(_ffn_body, norm_dtype=norm_dtype),
        grid=(rows // tm,),
        in_specs=in_specs,
        out_specs=out_specs,
        out_shape=out_shape,
        scratch_shapes=[pltpu.VMEM((tm, D_FF), BF16)],
        compiler_params=_cparams("parallel"),
        name="ffn_norm" if norm_dtype is not None else "ffn",
    )(*args)
    return res if norm_dtype is not None else res[0]


def _s5_discretise(lam_re, lam_im, log_dt, b_re, b_im):
    dt = jnp.exp(log_dt.astype(F32))[:, None]
    lr = lam_re.astype(F32)
    li = lam_im.astype(F32)
    mag = jnp.exp(lr * dt)
    a_re = mag * jnp.cos(li * dt)
    a_im = mag * jnp.sin(li * dt)
    den = lr * lr + li * li
    nr = a_re - 1.0
    z_re = (nr * lr + a_im * li) / den
    z_im = (a_im * lr - nr * li) / den
    br = b_re.astype(F32)
    bi = b_im.astype(F32)
    bb_re = z_re[..., None] * br - z_im[..., None] * bi
    bb_im = z_re[..., None] * bi + z_im[..., None] * br
    return a_re, a_im, bb_re, bb_im, lr * dt, li * dt


def _s5_chunk_tables(lam_re, lam_im, log_dt, b_re, b_im, c_re, c_im):
    L, G, N, C = S5_CHUNK, S5_GROUPS, S5_STATE, S5_GROUP
    _, _, bb_re, bb_im, ldr, ldi = _s5_discretise(lam_re, lam_im, log_dt, b_re, b_im)
    tau = jnp.arange(L + 1, dtype=F32)[:, None, None]
    mag = jnp.exp(ldr[None] * tau)
    p_re = mag * jnp.cos(ldi[None] * tau)
    p_im = mag * jnp.sin(ldi[None] * tau)
    cr = c_re.astype(F32)
    ci = c_im.astype(F32)
    hi = lax.Precision.HIGHEST
    ca_re = cr[None] * p_re[:, :, None, :] - ci[None] * p_im[:, :, None, :]
    ca_im = cr[None] * p_im[:, :, None, :] + ci[None] * p_re[:, :, None, :]
    ca_rows = jnp.concatenate([ca_re[:L], ca_im[:L]], axis=-1).transpose(1, 3, 0, 2).reshape(G, 2 * N, L * C)
    bb_rows = jnp.concatenate([bb_re, -bb_im], axis=1)
    resp = jnp.einsum('gki,gkx->gix', bb_rows, ca_rows, precision=hi)
    toep = jnp.stack([jnp.pad(resp[:, :, :(L - s) * C], ((0, 0), (0, 0), (s * C, 0))) for s in range(L)], axis=1)
    toep = toep.reshape(G, L * C, L * C)
    pw_re = p_re[L - 1 - jnp.arange(L)]
    pw_im = p_im[L - 1 - jnp.arange(L)]
    w_re = pw_re[..., None] * bb_re[None] - pw_im[..., None] * bb_im[None]
    w_im = pw_re[..., None] * bb_im[None] + pw_im[..., None] * bb_re[None]
    w_re = w_re.transpose(1, 0, 3, 2).reshape(G, L * C, N)
    w_im = w_im.transpose(1, 0, 3, 2).reshape(G, L * C, N)
    w_in = jnp.concatenate([w_re, w_im, w_im, w_re], axis=-1)
    v_re = ca_re[1:].transpose(1, 3, 0, 2).reshape(G, N, L * C)
    v_im = -ca_im[1:].transpose(1, 3, 0, 2).reshape(G, N, L * C)
    v_out = jnp.concatenate([v_re, v_im], axis=1)
    al_re, al_im = p_re[L], p_im[L]
    decay = jnp.stack([jnp.concatenate([al_re, al_re], -1),
                       jnp.concatenate([-al_im, al_im], -1),
                       jnp.concatenate([al_im, -al_im], -1)], axis=1)
    return toep.astype(BF16), w_in.astype(BF16), v_out.astype(BF16), decay


def _block_transpose(xs):
    n = len(xs)
    block = lax.broadcasted_iota(jnp.int32, xs[0].shape, 1) // S5_GROUP
    d = n // 2
    while d >= 1:
        upper = (block // d) % 2 == 1
        nxt = list(xs)
        for i in range(n):
            if i & d == 0:
                a, b = xs[i], xs[i | d]
                nxt[i] = jnp.where(upper, pltpu.roll(b, d * S5_GROUP, 1), a)
                nxt[i | d] = jnp.where(upper, b, pltpu.roll(a, LANES - d * S5_GROUP, 1))
        xs = nxt
        d //= 2
    return xs


def _s5_prompt_body(hn_ref, toep_ref, win_ref, vout_ref, dec_ref, y_ref, hfin_ref, u_scr, sy_scr, hp_scr, *, sps):
    lg = LANE_GROUPS
    rows = u_scr.shape[2]
    nch = rows // sps
    n2 = 2 * S5_STATE
    half = S5_CHUNK // 2
    rb = S5_RELAYOUT_ROWS
    blocks_per_seq = nch // rb

    def chunk_rows(blk):
        seq = blk // blocks_per_seq
        c0 = (blk % blocks_per_seq) * rb
        return c0 * sps + seq, (seq * nch + c0) * S5_CHUNK

    def gather(blk, _):
        r0, t0 = chunk_rows(blk)
        for h in range(2):
            xs = [hn_ref[pl.ds(t0 + h * half + tt, rb, stride=S5_CHUNK), :] for tt in range(half)]
            for g, y in enumerate(_block_transpose(xs)):
                u_scr[g, h, pl.ds(r0, rb, stride=sps), :] = y
        return _

    lax.fori_loop(0, sps * blocks_per_seq, gather, 0)

    def chunk_inputs(g):
        return jnp.concatenate([u_scr[g, 0], u_scr[g, 1]], axis=1).astype(BF16)

    for g in range(lg):
        s = _dot(chunk_inputs(g), win_ref[g])
        sy_scr[g, 0] = s[:, :n2]
        sy_scr[g, 1] = s[:, n2:]
    decs = [dec_ref[g] for g in range(lg)]
    cpv = SUBLANES // sps

    def step(i, carry):
        r0 = pl.multiple_of(i * SUBLANES, SUBLANES)
        new = []
        for g in range(lg):
            h, hs = carry[g]
            a1 = decs[g][0:1, :]
            a2 = decs[g][1:2, :]
            a2s = decs[g][2:3, :]
            slab = sy_scr[g, 0, pl.ds(r0, SUBLANES), :]
            slab_s = sy_scr[g, 1, pl.ds(r0, SUBLANES), :]
            prev = []
            for c in range(cpv):
                prev.append(h)
                rs = slice(c * sps, (c + 1) * sps)
                h, hs = (a1 * h + a2 * hs + slab[rs], a1 * hs + a2s * h + slab_s[rs])
            hp_scr[g, pl.ds(r0, SUBLANES), :] = jnp.concatenate(prev, axis=0)
            new.append((h, hs))
        return tuple(new)

    zero = jnp.zeros((sps, n2), F32)
    fin = lax.fori_loop(0, rows // SUBLANES, step, tuple((zero, zero) for _ in range(lg)))
    for g in range(lg):
        hfin_ref[0, g] = fin[g][0]
        y = _dot(chunk_inputs(g), toep_ref[g]) + _dot(hp_scr[g].astype(BF16), vout_ref[g])
        sy_scr[g, 0] = y[:, :n2]
        sy_scr[g, 1] = y[:, n2:]

    def scatter(blk, _):
        r0, t0 = chunk_rows(blk)
        for h in range(2):
            ys = [sy_scr[g, h, pl.ds(r0, rb, stride=sps), :] for g in range(lg)]
            for tt, x in enumerate(_block_transpose(ys)):
                y_ref[pl.ds(t0 + h * half + tt, rb, stride=S5_CHUNK), :] = x
        return _

    lax.fori_loop(0, sps * blocks_per_seq, scatter, 0)


def _s5_prompt(hn, tables, bsz):
    toep, w_in, v_out, decay = tables
    G, L, N = S5_GROUPS, S5_CHUNK, S5_STATE
    t = hn.shape[0] // bsz
    sps = S5_SEQS_PER_STEP
    assert bsz % sps == 0 and SUBLANES % sps == 0 and (t // L) % S5_RELAYOUT_ROWS == 0
    rows = (t // L) * sps
    lg = LANE_GROUPS
    lc = L * S5_GROUP
    tok_spec = pl.BlockSpec((sps * t, LANES), lambda j, s: (s, j))
    gspec = lambda *shape: pl.BlockSpec((lg,) + shape, lambda j, s: (j,) + (0,) * len(shape))
    y, hfin = pl.pallas_call(
        functools.partial(_s5_prompt_body, sps=sps),
        grid=(G // lg, bsz // sps),
        in_specs=[tok_spec, gspec(lc, lc), gspec(lc, lc), gspec(2 * N, lc), gspec(3, 2 * N)],
        out_specs=[tok_spec, pl.BlockSpec((1, lg, sps, 2 * N), lambda j, s: (s, j, 0, 0))],
        out_shape=[jax.ShapeDtypeStruct(hn.shape, F32), jax.ShapeDtypeStruct((bsz // sps, G, sps, 2 * N), F32)],
        scratch_shapes=[pltpu.VMEM((lg, 2, rows, LANES), F32), pltpu.VMEM((lg, 2, rows, LANES), F32),
                        pltpu.VMEM((lg, rows, 2 * N), F32)],
        compiler_params=_cparams("parallel", "parallel"),
        name="s5_prompt",
    )(hn, toep, w_in, v_out, decay)
    hfin = hfin.transpose(0, 2, 1, 3).reshape(bsz, G, 2 * N)
    return y, hfin[:, :, :N], hfin[:, :, N:]


def _s5_sample_body(u_ref, bbr_ref, bbi_ref, ar_ref, ai_ref, h0r_ref, h0i_ref, cr_ref, ci_ref,
                    y_ref, hr_ref, hi_ref):
    hi = lax.Precision.HIGHEST
    u = u_ref[...]
    bu_re = jnp.einsum('gbc,gnc->gbn', u, bbr_ref[...], precision=hi, preferred_element_type=F32)
    bu_im = jnp.einsum('gbc,gnc->gbn', u, bbi_ref[...], precision=hi, preferred_element_type=F32)
    ar, ai = ar_ref[...], ai_ref[...]
    h0r, h0i = h0r_ref[...], h0i_ref[...]
    hr = bu_re + ar * h0r - ai * h0i
    hi_ = bu_im + ar * h0i + ai * h0r
    hr_ref[...] = hr
    hi_ref[...] = hi_
    y_ref[...] = (jnp.einsum('gbn,gcn->gbc', hr, cr_ref[...], precision=hi, preferred_element_type=F32)
                  - jnp.einsum('gbn,gcn->gbc', hi_, ci_ref[...], precision=hi, preferred_element_type=F32))


def _s5_sample(hs, h0_re, h0_im, lam_re, lam_im, log_dt, b_re, b_im, c_re, c_im):
    G, C, N = S5_GROUPS, S5_GROUP, S5_STATE
    b = hs.shape[0]
    a_re, a_im, bb_re, bb_im, _, _ = _s5_discretise(lam_re, lam_im, log_dt, b_re, b_im)
    u = hs.reshape(b, G, C).transpose(1, 0, 2)
    args = [u, bb_re, bb_im, a_re[:, None, :], a_im[:, None, :],
            h0_re.astype(F32).transpose(1, 0, 2), h0_im.astype(F32).transpose(1, 0, 2),
            c_re.astype(F32), c_im.astype(F32)]
    full = lambda a: pl.BlockSpec(a.shape, lambda i: (0,) * a.ndim)
    outs = [jax.ShapeDtypeStruct((G, b, C), F32), jax.ShapeDtypeStruct((G, b, N), F32),
            jax.ShapeDtypeStruct((G, b, N), F32)]
    y, hr, hi = pl.pallas_call(
        _s5_sample_body,
        grid=(1,),
        in_specs=[full(a) for a in args],
        out_specs=[full(o) for o in outs],
        out_shape=outs,
        compiler_params=_cparams("arbitrary"),
        name="s5_sample",
    )(*args)
    return y.transpose(1, 0, 2).reshape(b, D_MODEL), hr.transpose(1, 0, 2), hi.transpose(1, 0, 2)


def _s5_post_body(x_ref, y_ref, gm_ref, d_ref, w_ref, b_ref, o_ref):
    x = x_ref[...]
    hp = _rms(x, gm_ref[...])
    z = jax.nn.gelu(y_ref[...] + d_ref[...] * hp)
    zz = _dot(z.astype(BF16), w_ref[...]) + b_ref[...]
    o_ref[...] = x + zz[:, :D_MODEL] * jax.nn.sigmoid(zz[:, D_MODEL:])


def _s5_post(x, y, g_mix, d, w_glu, b_glu):
    rows = x.shape[0]
    tm = _row_tile(rows)
    row_spec = pl.BlockSpec((tm, D_MODEL), lambda i: (i, 0))
    return pl.pallas_call(
        _s5_post_body,
        grid=(rows // tm,),
        in_specs=[row_spec, row_spec, _resident((1, D_MODEL)), _resident((1, D_MODEL)),
                  _resident((D_MODEL, 2 * D_MODEL)), _resident((1, 2 * D_MODEL))],
        out_specs=row_spec,
        out_shape=jax.ShapeDtypeStruct((rows, D_MODEL), F32),
        compiler_params=_cparams("parallel"),
        name="s5_post",
    )(x, y, g_mix.reshape(1, D_MODEL), d.astype(F32).reshape(1, D_MODEL), w_glu, b_glu.astype(F32).reshape(1, -1))


def _rope_tables(pos):
    half = HEAD_DIM // 2
    inv = ROPE_THETA ** (-jnp.arange(half, dtype=F32) / half)
    ang = pos.astype(F32)[:, None] * inv[None, :]
    cos = jnp.cos(ang)
    sin = jnp.sin(ang)
    cos_t = jnp.concatenate([cos, cos, cos, cos], axis=-1)
    sin_t = jnp.concatenate([-sin, sin, -sin, sin], axis=-1)
    return cos_t, sin_t


def _head_sum_matrix():
    head = jnp.arange(D_MODEL) // HEAD_DIM
    return (head[:, None] == head[None, :]).astype(BF16)


def _qkv_body(hn_ref, w_ref, qg_ref, kg_ref, hsum_ref, cos_ref, sin_ref,
              q_ref, kf_ref, kb_ref, vf_ref, vb_ref, *, seq_minor_k):
    hn = hn_ref[...]
    cos = cos_ref[...]
    sin = sin_ref[...]
    lane = lax.broadcasted_iota(jnp.int32, cos.shape, 1)
    low_half = (lane % HEAD_DIM) < (HEAD_DIM // 2)

    def norm_rope(x, g):
        ss = _dot((x * x).astype(BF16), hsum_ref[...])
        y = x * lax.rsqrt(ss * (1.0 / HEAD_DIM) + EPS) * g
        outs = []
        for j in range(D_MODEL // LANES):
            t = y[:, j * LANES:(j + 1) * LANES]
            below = pltpu.roll(t, HEAD_DIM // 2, 1)
            above = pltpu.roll(t, LANES - HEAD_DIM // 2, 1)
            outs.append(t * cos + jnp.where(low_half, above, below) * sin)
        return jnp.concatenate(outs, axis=1)

    q = norm_rope(_dot(hn, w_ref[:, 0:D_MODEL]), qg_ref[...])
    q_ref[...] = (q * ATTN_SCALE).astype(BF16)
    k = norm_rope(_dot(hn, w_ref[:, D_MODEL:2 * D_MODEL]), kg_ref[...])
    if seq_minor_k:
        kf_ref[0] = k.T.reshape(kf_ref.shape[1:])
    else:
        kf_ref[...] = k
    kb_ref[...] = k.astype(BF16)
    v = _dot(hn, w_ref[:, 2 * D_MODEL:3 * D_MODEL])
    vf_ref[...] = v
    vb_ref[...] = v.astype(BF16)


def _qkv(hn, w_qkv, q_norm, k_norm, cos_t, sin_t, seq_minor_k=False):
    rows = hn.shape[0]
    tm = _row_tile(rows)
    tab_blocks = cos_t.shape[0] // tm
    row_spec = pl.BlockSpec((tm, D_MODEL), lambda i: (i, 0))
    tab_spec = pl.BlockSpec((tm, LANES), lambda i: (i % tab_blocks, 0))
    gain = lambda g: jnp.tile(g.astype(F32), N_SUB).reshape(1, D_MODEL)
    outs = [jax.ShapeDtypeStruct((rows, D_MODEL), dt) for dt in (BF16, F32, BF16, F32, BF16)]
    out_specs = [row_spec] * 5
    if seq_minor_k:
        period = cos_t.shape[0]
        outs[1] = jax.ShapeDtypeStruct((rows // period, N_SUB, HEAD_DIM, period), F32)
        out_specs[1] = pl.BlockSpec((1, N_SUB, HEAD_DIM, tm), lambda i: (i // tab_blocks, 0, 0, i % tab_blocks))
    return pl.pallas_call(
        functools.partial(_qkv_body, seq_minor_k=seq_minor_k),
        grid=(rows // tm,),
        in_specs=[row_spec, _resident((D_MODEL, 3 * D_MODEL)), _resident((1, D_MODEL)), _resident((1, D_MODEL)),
                  _resident((D_MODEL, D_MODEL)), tab_spec, tab_spec],
        out_specs=out_specs,
        out_shape=outs,
        compiler_params=_cparams("parallel"),
        name="qkv",
    )(hn, w_qkv, gain(q_norm), gain(k_norm), _head_sum_matrix(), cos_t, sin_t)


def _lambda_full(lq1_ref, lk1_ref, lq2_ref, lk2_ref, lam_init):
    s1 = jnp.sum(lq1_ref[...] * lk1_ref[...], axis=-1, keepdims=True)
    s2 = jnp.sum(lq2_ref[...] * lk2_ref[...], axis=-1, keepdims=True)
    return jnp.exp(s1) - jnp.exp(s2) + lam_init


def _merge(o1, o2, lam, subln, lam_init):
    out = o1 - lam * o2
    return _rms(out, subln) * (1.0 - lam_init)


def _attn_prompt_body(q_ref, k_ref, v_ref, lq1_ref, lk1_ref, lq2_ref, lk2_ref, subln_ref, o_ref, *, lam_init, tk, nq):
    tq = q_ref.shape[0]
    qi = pl.program_id(2)
    kt_per_q = tq // tk
    qt = q_ref[...].astype(F32).T.astype(BF16)
    dim = lax.broadcasted_iota(jnp.int32, qt.shape, 0)
    zero = jnp.zeros_like(qt)
    qst = jnp.concatenate([jnp.where(dim < HEAD_DIM, qt, zero), jnp.where(dim >= HEAD_DIM, qt, zero)], axis=1)

    span = MXU_TILE
    n_span = 2 * tq // span

    def visible_keys(c, diagonal):
        if diagonal is None:
            return tk
        return max(0, min(tk, (c * span) % tq + span - diagonal * tk))

    def scores(j, diagonal):
        out = []
        for c in range(n_span):
            nk = visible_keys(c, diagonal)
            out.append(_dot(k_ref[pl.ds(j * tk, nk), :], qst[:, c * span:(c + 1) * span]) if nk else None)
        return out

    def absorb(j, tile_scores, carry, diagonal):
        new = []
        for c, (s, (m, l, acc)) in enumerate(zip(tile_scores, carry)):
            if s is None:
                new.append((m, l, acc))
                continue
            if diagonal is not None:
                key = lax.broadcasted_iota(jnp.int32, s.shape, 0) + diagonal * tk
                qry = lax.broadcasted_iota(jnp.int32, s.shape, 1) + (c * span) % tq
                s = jnp.where(key <= qry, s, NEG_INF)
            m_new = jnp.maximum(m, jnp.max(s, axis=0, keepdims=True))
            alpha = jnp.exp(m - m_new)
            p = jnp.exp(s - m_new)
            l_new = l * alpha + jnp.sum(p, axis=0, keepdims=True)
            vj = v_ref[pl.ds(j * tk, s.shape[0]), :]
            pv = lax.dot_general(vj, p.astype(BF16), (((0,), (0,)), ((), ())), preferred_element_type=F32)
            new.append((m_new, l_new, acc * alpha + pv))
        return new

    def run(n_tiles):
        carry = [(jnp.full((1, span), NEG_INF, F32), jnp.zeros((1, span), F32), jnp.zeros((V_DIM, span), F32))
                 for _ in range(n_span)]
        def diagonal(j):
            d = j - (n_tiles - kt_per_q)
            return d if d >= 0 else None

        tile_scores = scores(0, diagonal(0))
        for j in range(n_tiles):
            next_scores = scores(j + 1, diagonal(j + 1)) if j + 1 < n_tiles else None
            carry = absorb(j, tile_scores, carry, diagonal(j))
            tile_scores = next_scores
        o = jnp.concatenate([acc / l for _, l, acc in carry], axis=1)
        lam = _lambda_full(lq1_ref, lk1_ref, lq2_ref, lk2_ref, lam_init)
        out = o[:, :tq] - lam * o[:, tq:]
        out = out * lax.rsqrt(jnp.mean(out * out, axis=0, keepdims=True) + EPS) * subln_ref[...] * (1.0 - lam_init)
        o_ref[...] = out.T.astype(BF16)

    for n in range(nq):
        pl.when(qi == n)(functools.partial(run, (n + 1) * kt_per_q))


def _attn_prompt(q, k, v, lams, subln, lam_init, bsz):
    rows = q.shape[0]
    t = rows // bsz
    tq = ATTN_Q_TILE if t % ATTN_Q_TILE == 0 else t
    tk = ATTN_K_TILE if tq % ATTN_K_TILE == 0 else tq
    nq = t // tq
    small = pl.BlockSpec((1, HEAD_DIM), lambda b, h, i: (0, 0))
    return pl.pallas_call(
        functools.partial(_attn_prompt_body, lam_init=lam_init, tk=tk, nq=nq),
        grid=(bsz, N_HEADS, nq),
        in_specs=[pl.BlockSpec((tq, V_DIM), lambda b, h, i: (b * nq + i, h)),
                  pl.BlockSpec((t, V_DIM), lambda b, h, i: (b, h)),
                  pl.BlockSpec((t, V_DIM), lambda b, h, i: (b, h)),
                  small, small, small, small,
                  pl.BlockSpec((V_DIM, 1), lambda b, h, i: (0, 0))],
        out_specs=pl.BlockSpec((tq, V_DIM), lambda b, h, i: (b * nq + i, h)),
        out_shape=jax.ShapeDtypeStruct((rows, D_MODEL), BF16),
        compiler_params=_cparams("parallel", "parallel", "arbitrary"),
        name="attn_prompt",
    )(q, k, v, *[a.astype(F32).reshape(1, HEAD_DIM) for a in lams], subln.astype(F32).reshape(V_DIM, 1))


def _pages_scores(qr, keys_t, n):
    return _dot(qr, jnp.concatenate([keys_t(i) for i in range(n)], axis=1))


def _pages_absorb(s, values, n, m_ref, l_ref, acc_ref):
    m_old = m_ref[0][:, 0:1]
    l_old = l_ref[0][:, 0:1]
    m_new = jnp.maximum(m_old, jnp.max(s, axis=-1, keepdims=True))
    alpha = jnp.exp(m_old - m_new)
    p = jnp.exp(s - m_new)
    l_new = l_old * alpha + jnp.sum(p, axis=-1, keepdims=True)
    pb = p.astype(BF16)
    pv = _dot(pb, jnp.concatenate([values(i) for i in range(n)], axis=0))
    acc_ref[0] = acc_ref[0] * alpha + pv
    m_ref[0] = jnp.broadcast_to(m_new, m_ref.shape[1:])
    l_ref[0] = jnp.broadcast_to(l_new, l_ref.shape[1:])


def _ffn_pages_body(pt_ref, x_ref, g_ref, win_ref, wout_ref, *rest, norm_dtype, layer, seq0, steps_per_seq, groups):
    if norm_dtype is None:
        qr_ref, ck_ref, cv_ref, o_ref, m_ref, l_ref, acc_ref, act_ref, kbuf, vbuf, sem = rest
    else:
        gm_ref, qr_ref, ck_ref, cv_ref, o_ref, hn_ref, m_ref, l_ref, acc_ref, act_ref, kbuf, vbuf, sem = rest
    i = pl.program_id(0)
    last = pl.num_programs(0) - 1
    gp = HOSTED_GROUP_PAGES
    page = kbuf.shape[-1]

    ring = kbuf.shape[0]
    KEYS, VALUES = 0, 1
    ahead = {KEYS: ring, VALUES: ring - 1}

    def copies(kind, step, grp):
        seq = seq0 + step // steps_per_seq
        first = (step % steps_per_seq) * (groups * gp) + grp * gp
        src, dst = (ck_ref, kbuf) if kind == KEYS else (cv_ref, vbuf)
        slot = grp % ring
        return [pltpu.make_async_copy(src.at[layer, pt_ref[seq, first + pg]], dst.at[slot, pg], sem.at[slot, kind])
                for pg in range(gp)]

    def start(kind, grp):
        if grp < groups:
            for n, c in enumerate(copies(kind, i, grp)):
                c.start(priority=n % 2)
        else:
            @pl.when(i < last)
            def _():
                for n, c in enumerate(copies(kind, i + 1, grp - groups)):
                    c.start(priority=n % 2)

    def wait(kind, grp):
        for c in copies(kind, i, grp):
            c.wait()

    @pl.when(i == 0)
    def _():
        for kind in (KEYS, VALUES):
            for grp in range(ahead[kind]):
                start(kind, grp)

    @pl.when(i % steps_per_seq == 0)
    def _():
        m_ref[...] = jnp.full(m_ref.shape, NEG_INF, F32)
        l_ref[...] = jnp.zeros(l_ref.shape, F32)
        acc_ref[...] = jnp.zeros(acc_ref.shape, F32)

    x = x_ref[...]
    xn = _rms(x, g_ref[...]).astype(BF16)
    qr = qr_ref[0]
    n_chunks = D_FF // FF_CHUNK

    def scores(grp):
        slot = grp % ring
        return _pages_scores(qr, lambda pg: kbuf[slot, pg].reshape(D_MODEL, page).astype(BF16), gp)

    def values(grp):
        slot = grp % ring

        def page_values(pg):
            heads = [vbuf[slot, pg, pl.ds(h, page, stride=N_HEADS), :] for h in range(N_HEADS)]
            return jnp.concatenate(heads, axis=1).astype(BF16)
        return page_values

    wait(KEYS, 0)
    s = scores(0)
    for grp in range(groups):
        start(KEYS, grp + ahead[KEYS])
        start(VALUES, grp + ahead[VALUES])
        if grp + 1 < groups:
            wait(KEYS, grp + 1)
        wait(VALUES, grp)
        for c in range(grp * n_chunks // groups, (grp + 1) * n_chunks // groups):
            lo = c * FF_CHUNK
            g = _dot(xn, win_ref[:, lo:lo + FF_CHUNK])
            u = _dot(xn, win_ref[:, D_FF + lo:D_FF + lo + FF_CHUNK])
            act_ref[:, lo:lo + FF_CHUNK] = (g * jax.nn.sigmoid(g) * u).astype(BF16)
        s_next = scores(grp + 1) if grp + 1 < groups else None
        _pages_absorb(s, values(grp), gp, m_ref, l_ref, acc_ref)
        s = s_next
    xo = x + 0.5 * _dot(act_ref[...], wout_ref[...])
    o_ref[...] = xo
    if norm_dtype is not None:
        hn_ref[...] = _rms(xo, gm_ref[...]).astype(norm_dtype)


def _ffn_pages(x, g, w_in, w_out, q_rows, cache_k_t, cache_v_rows, layer, page_table, seq0, n_seq,
               g_mix=None, norm_dtype=None):
    rows = x.shape[0]
    tm = ROW_TILE
    steps = rows // tm
    n_pages = page_table.shape[1]
    page = cache_k_t.shape[-1]
    assert rows % tm == 0 and steps % n_seq == 0
    steps_per_seq = steps // n_seq
    groups = n_pages // steps_per_seq // HOSTED_GROUP_PAGES
    assert groups * HOSTED_GROUP_PAGES * steps_per_seq == n_pages and groups % HOSTED_RING == 0
    row_spec = pl.BlockSpec((tm, D_MODEL), lambda i, pt: (i, 0))
    res = lambda shape: pl.BlockSpec(shape, lambda i, pt: (0,) * len(shape), pipeline_mode=pl.Buffered(1))
    seq_spec = lambda w, off: pl.BlockSpec((1, N_SUB, w), lambda i, pt: (off + i // steps_per_seq, 0, 0))
    hbm = pl.BlockSpec(memory_space=pl.ANY)
    in_specs = [row_spec, res((1, D_MODEL)), res((D_MODEL, 2 * D_FF)), res((D_FF, D_MODEL))]
    args = [x, g.reshape(1, D_MODEL), w_in, w_out]
    out_shape = [jax.ShapeDtypeStruct((rows, D_MODEL), F32)]
    out_specs = [row_spec]
    if norm_dtype is not None:
        in_specs.append(res((1, D_MODEL)))
        args.append(g_mix.reshape(1, D_MODEL))
        out_shape.append(jax.ShapeDtypeStruct((rows, D_MODEL), norm_dtype))
        out_specs.append(row_spec)
    in_specs += [seq_spec(D_MODEL, seq0), hbm, hbm]
    args += [q_rows, cache_k_t, cache_v_rows]
    out_shape += [jax.ShapeDtypeStruct((n_seq, N_SUB, LANES), F32), jax.ShapeDtypeStruct((n_seq, N_SUB, LANES), F32),
                  jax.ShapeDtypeStruct((n_seq, N_SUB, D_MODEL), F32)]
    out_specs += [seq_spec(LANES, 0), seq_spec(LANES, 0), seq_spec(D_MODEL, 0)]
    gpp = HOSTED_GROUP_PAGES
    grid_spec = pltpu.PrefetchScalarGridSpec(
        num_scalar_prefetch=1,
        grid=(steps,),
        in_specs=in_specs,
        out_specs=out_specs,
        scratch_shapes=[pltpu.VMEM((tm, D_FF), BF16),
                        pltpu.VMEM((HOSTED_RING, gpp, N_SUB, HEAD_DIM, page), F32),
                        pltpu.VMEM((HOSTED_RING, gpp, page * N_HEADS, V_DIM), F32),
                        pltpu.SemaphoreType.DMA((HOSTED_RING, 2))],
    )
    return pl.pallas_call(
        functools.partial(_ffn_pages_body, norm_dtype=norm_dtype, layer=layer, seq0=seq0,
                          steps_per_seq=steps_per_seq, groups=groups),
        grid_spec=grid_spec,
        out_shape=out_shape,
        compiler_params=_cparams("arbitrary", vmem=VMEM_LIMIT_HOSTED),
        name="ffn_pages_norm" if norm_dtype is not None else "ffn_pages",
    )(page_table, *args)


def _attn_sample_finish_body(qr_ref, kn_ref, vn_ref, m_ref, l_ref, acc_ref,
                             lq1_ref, lk1_ref, lq2_ref, lk2_ref, subln_ref, o_ref, *, lam_init):
    qr = qr_ref[0].astype(F32)
    kn = kn_ref[0].astype(F32)
    vn = vn_ref[0].astype(F32)
    s = jnp.sum(qr * kn, axis=-1, keepdims=True)
    m_old = m_ref[0][:, 0:1]
    l_old = l_ref[0][:, 0:1]
    m_new = jnp.maximum(m_old, s)
    alpha = jnp.exp(m_old - m_new)
    p = jnp.exp(s - m_new)
    l_new = l_old * alpha + p
    acc = acc_ref[0] * alpha + p.astype(BF16).astype(F32) * vn
    o = acc / l_new
    lam = _lambda_full(lq1_ref, lk1_ref, lq2_ref, lk2_ref, lam_init)
    heads = []
    for h in range(N_HEADS):
        lanes = slice(h * V_DIM, (h + 1) * V_DIM)
        heads.append(_merge(o[2 * h:2 * h + 1, lanes], o[2 * h + 1:2 * h + 2, lanes], lam, subln_ref[...], lam_init))
    o_ref[0] = jnp.concatenate(heads, axis=1).astype(BF16)


def _attn_sample_finish(q_rows, k_new, v_new, m, l, acc, lams, subln, lam_init):
    b = q_rows.shape[0]
    seq3 = lambda w: pl.BlockSpec((1, 1, w), lambda s: (s, 0, 0))
    sub3 = lambda w: pl.BlockSpec((1, N_SUB, w), lambda s: (s, 0, 0))
    small = pl.BlockSpec((1, HEAD_DIM), lambda s: (0, 0))
    out = pl.pallas_call(
        functools.partial(_attn_sample_finish_body, lam_init=lam_init),
        grid=(b,),
        in_specs=[sub3(D_MODEL), seq3(D_MODEL), seq3(D_MODEL), sub3(LANES), sub3(LANES), sub3(D_MODEL),
                  small, small, small, small, pl.BlockSpec((1, V_DIM), lambda s: (0, 0))],
        out_specs=seq3(D_MODEL),
        out_shape=jax.ShapeDtypeStruct((b, 1, D_MODEL), BF16),
        compiler_params=_cparams("parallel"),
        name="attn_sample_finish",
    )(q_rows, k_new.reshape(b, 1, D_MODEL), v_new.reshape(b, 1, D_MODEL), m, l, acc,
      *[a.astype(F32).reshape(1, HEAD_DIM) for a in lams], subln.astype(F32).reshape(1, V_DIM))
    return out.reshape(b, D_MODEL)


def _out_proj_body(x_ref, a_ref, w_ref, o_ref):
    o_ref[...] = x_ref[...] + _dot(a_ref[...], w_ref[...])


def _out_proj(x, a, w_o):
    rows = x.shape[0]
    tm = _row_tile(rows)
    row_spec = pl.BlockSpec((tm, D_MODEL), lambda i: (i, 0))
    return pl.pallas_call(
        _out_proj_body,
        grid=(rows // tm,),
        in_specs=[row_spec, row_spec, _resident((D_MODEL, D_MODEL))],
        out_specs=row_spec,
        out_shape=jax.ShapeDtypeStruct((rows, D_MODEL), F32),
        compiler_params=_cparams("parallel"),
        name="out_proj",
    )(x, a, w_o)


def _block_diag_rows(q):
    head = jnp.arange(D_MODEL) // HEAD_DIM
    keep = head[None, :] == jnp.arange(N_SUB)[:, None]
    return jnp.where(keep[None], q[:, None, :], jnp.zeros((), q.dtype))


def kernel(x_prompt, x_sample, state_s5_re, state_s5_im, cache_k, cache_v, page_table, ffn_norm, ffn_w_in, ffn_w_out, mix_norm, s5_lambda_re, s5_lambda_im, s5_log_dt, s5_b_re, s5_b_im, s5_c_re, s5_c_im, s5_d, s5_w_glu, s5_b_glu, attn_w_qkv, attn_q_norm, attn_k_norm, attn_lambda_q1, attn_lambda_k1, attn_lambda_q2, attn_lambda_k2, attn_subln, attn_w_o):
    bsz, seq, _ = x_prompt.shape
    dbs, dseq, _ = x_sample.shape
    assert dseq == 1, "the sample group decodes one token per sequence"
    assert SUBLANES % bsz == 0 and seq % S5_CHUNK == 0
    depth = ffn_norm.shape[0]
    past_len = page_table.shape[1] * cache_k.shape[2]

    xp = x_prompt.reshape(bsz * seq, D_MODEL)
    xs = x_sample.reshape(dbs * dseq, D_MODEL)
    w_in = ffn_w_in.astype(BF16)
    w_out = ffn_w_out.astype(BF16)
    f32n = ffn_norm.astype(F32)
    mixn = mix_norm.astype(F32)

    s5_re_p, s5_im_p, s5_re_s, s5_im_s = [], [], [], []
    k_p, v_p, k_s, v_s = [], [], [], []
    for i in range(depth):
        a = i // N_MIXERS
        if i % N_MIXERS == 0:
            xp, hp = _ffn(xp, f32n[i, 0], w_in[i, 0], w_out[i, 0], mixn[i], F32)
            xs, hs = _ffn(xs, f32n[i, 0], w_in[i, 0], w_out[i, 0], mixn[i], F32)
            prm = (s5_lambda_re[a], s5_lambda_im[a], s5_log_dt[a], s5_b_re[a], s5_b_im[a], s5_c_re[a], s5_c_im[a])
            w_glu = s5_w_glu[a].astype(BF16)
            yp, hr_p, hi_p = _s5_prompt(hp, _s5_chunk_tables(*prm), bsz)
            ys, hr_s, hi_s = _s5_sample(hs, state_s5_re[a], state_s5_im[a], *prm)
            xp = _s5_post(xp, yp, mixn[i], s5_d[a], w_glu, s5_b_glu[a])
            xs = _s5_post(xs, ys, mixn[i], s5_d[a], w_glu, s5_b_glu[a])
            s5_re_p.append(hr_p)
            s5_im_p.append(hi_p)
            s5_re_s.append(hr_s)
            s5_im_s.append(hi_s)
        else:
            lam_init = 0.8 - 0.6 * math.exp(-0.3 * i)
            lams = (attn_lambda_q1[a], attn_lambda_k1[a], attn_lambda_q2[a], attn_lambda_k2[a])
            w_qkv = attn_w_qkv[a].astype(BF16)
            w_o = attn_w_o[a].astype(BF16)
            cos_p, sin_p = _rope_tables(jnp.arange(seq))
            cos_s, sin_s = _rope_tables(jnp.tile(past_len + jnp.arange(dseq), dbs))
            xs, hs = _ffn(xs, f32n[i, 0], w_in[i, 0], w_out[i, 0], mixn[i], BF16)
            qs, ksf, ksb, vsf, vsb = _qkv(hs, w_qkv, attn_q_norm[a], attn_k_norm[a], cos_s, sin_s)
            q_rows = _block_diag_rows(qs)
            ck_t = jnp.transpose(cache_k, (0, 1, 3, 4, 2))
            cv_rows = cache_v.reshape(cache_v.shape[:2] + (cache_v.shape[2] * N_HEADS, V_DIM))
            first = dbs // 2
            xp, hp, m0, l0, acc0 = _ffn_pages(xp, f32n[i, 0], w_in[i, 0], w_out[i, 0], q_rows, ck_t, cv_rows, a,
                                              page_table, 0, first, mixn[i], BF16)
            qp, kpf, kpb, vpf, vpb = _qkv(hp, w_qkv, attn_q_norm[a], attn_k_norm[a], cos_p, sin_p, seq_minor_k=True)
            op = _attn_prompt(qp, kpb, vpb, lams, attn_subln[a], lam_init, bsz)
            xp = _out_proj(xp, op, w_o)
            xp, m1, l1, acc1 = _ffn_pages(xp, f32n[i, 1], w_in[i, 1], w_out[i, 1], q_rows, ck_t, cv_rows, a,
                                          page_table, first, dbs - first)
            m, l, acc = (jnp.concatenate(parts, axis=0) for parts in ((m0, m1), (l0, l1), (acc0, acc1)))
            os_ = _attn_sample_finish(q_rows, ksb, vsb, m, l, acc, lams, attn_subln[a], lam_init)
            xs = _out_proj(xs, os_, w_o)
            xs = _ffn(xs, f32n[i, 1], w_in[i, 1], w_out[i, 1])
            k_p.append(kpf.transpose(0, 3, 1, 2))
            v_p.append(vpf.reshape(bsz, seq, N_HEADS, V_DIM))
            k_s.append(ksf.reshape(dbs, dseq, N_SUB, HEAD_DIM))
            v_s.append(vsf.reshape(dbs, dseq, N_HEADS, V_DIM))
            continue
        xp = _ffn(xp, f32n[i, 1], w_in[i, 1], w_out[i, 1])
        xs = _ffn(xs, f32n[i, 1], w_in[i, 1], w_out[i, 1])
    return (xp.reshape(bsz, seq, D_MODEL), xs.reshape(dbs, dseq, D_MODEL),
            jnp.stack(s5_re_p), jnp.stack(s5_im_p), jnp.stack(s5_re_s), jnp.stack(s5_im_s),
            jnp.stack(k_p), jnp.stack(v_p), jnp.stack(k_s), jnp.stack(v_s))
```
